```python
import jax, jax.numpy as jnp
from jax import lax
import numpy as np

D_MODEL = 4096
BATCH = 16
SEQ = 256
DEPTH = 4
DEC_BATCH = 4
DEC_SEQ = 1024
PAST_LEN = 512

GRID_W = 64
HEAD_DIM = 128
N_HEADS = 24
N_KV_HEADS = 8
Q_PER_KV = N_HEADS // N_KV_HEADS
ATTN_WIDTH = N_HEADS * HEAD_DIM
KV_WIDTH = N_KV_HEADS * HEAD_DIM
N_FOURIER_GROUPS = 8
FOURIER_GROUP_DIM = 128
FOURIER_WIDTH = N_FOURIER_GROUPS * FOURIER_GROUP_DIM
MIX_WIDTH = ATTN_WIDTH + FOURIER_WIDTH
IN_WIDTH = ATTN_WIDTH + 2 * KV_WIDTH + FOURIER_WIDTH
N_EXPERTS = 16
N_EXPERT_GROUPS = 4
EXPERTS_PER_GROUP = N_EXPERTS // N_EXPERT_GROUPS
TOP_K = 2
D_EXPERT = 2048
Q_BLOCK = 128
ROPE_THETA = 10000.0
ROPE_AXIS_DIM = HEAD_DIM // 2
EPS = 1e-6
N_MOD = 6

kernel_name = "hybrid_gqa_fourier_sharedrouter_moe_diffusion_step"


def rms_norm(x, g):
    xf = x.astype(jnp.float32)
    y = xf * lax.rsqrt(jnp.mean(xf * xf, axis=-1, keepdims=True) + EPS)
    return (y * g).astype(x.dtype)


def axial_rope_tables(n_tok):
    rows = n_tok // GRID_W
    row_idx = jnp.repeat(jnp.arange(rows, dtype=jnp.float32), GRID_W)
    col_idx = jnp.tile(jnp.arange(GRID_W, dtype=jnp.float32), rows)
    freqs = ROPE_THETA ** (-jnp.arange(0, ROPE_AXIS_DIM, 2, dtype=jnp.float32) / ROPE_AXIS_DIM)
    ang = jnp.concatenate([row_idx[:, None] * freqs, col_idx[:, None] * freqs], axis=-1)
    return jnp.cos(ang), jnp.sin(ang)


def apply_rope(x, cos, sin):
    b, s, h, d = x.shape
    xr = x.astype(jnp.float32).reshape(b, s, h, d // 2, 2)
    x0, x1 = xr[..., 0], xr[..., 1]
    cs, sn = cos[None, :, None, :], sin[None, :, None, :]
    out = jnp.stack([x0 * cs - x1 * sn, x0 * sn + x1 * cs], axis=-1)
    return out.reshape(b, s, h, d).astype(x.dtype)


def block_attention(q, k, v):
    b, s = q.shape[0], q.shape[1]
    nb = s // Q_BLOCK
    qb = q.reshape(b, nb, Q_BLOCK, N_KV_HEADS, Q_PER_KV, HEAD_DIM).transpose(1, 0, 2, 3, 4, 5)
    scale = HEAD_DIM ** -0.5

    def one_block(qblk):
        sc = jnp.einsum('bqkgd,blkd->bkgql', qblk, k, preferred_element_type=jnp.float32) * scale
        p = jax.nn.softmax(sc, axis=-1)
        return jnp.einsum('bkgql,blkd->bqkgd', p.astype(v.dtype), v)

    out = lax.map(one_block, qb)
    return out.transpose(1, 0, 2, 3, 4, 5).reshape(b, s, ATTN_WIDTH)


def fourier_mix(u, w_f):
    b, s, _ = u.shape
    ug = u.reshape(b, s, N_FOURIER_GROUPS, FOURIER_GROUP_DIM).astype(jnp.float32)
    f = jnp.fft.fft2(ug, axes=(1, 3), norm="ortho").real
    o = jnp.einsum('bsgc,gcd->bsgd', f.astype(u.dtype), w_f)
    return o.reshape(b, s, FOURIER_WIDTH)


def project(h, w_in, q_g, k_g):
    b, s, _ = h.shape
    p = jnp.einsum('bsd,de->bse', h, w_in)
    q, k, v, u = jnp.split(p, [ATTN_WIDTH, ATTN_WIDTH + KV_WIDTH, ATTN_WIDTH + 2 * KV_WIDTH], axis=-1)
    q = rms_norm(q.reshape(b, s, N_HEADS, HEAD_DIM), q_g)
    k = rms_norm(k.reshape(b, s, N_KV_HEADS, HEAD_DIM), k_g)
    v = v.reshape(b, s, N_KV_HEADS, HEAD_DIM)
    return q, k, v, u


def context_mixer(h, w_in, q_g, k_g, w_f, w_o):
    q, k, v, u = project(h, w_in, q_g, k_g)
    attn = block_attention(q, k, v)
    mix = jnp.concatenate([attn, fourier_mix(u, w_f)], axis=-1)
    return jnp.einsum('bse,ed->bsd', mix, w_o), k, v


def latent_mixer(h, k_ctx, v_ctx, w_in, q_g, k_g, w_f, w_o):
    q, k, v, u = project(h, w_in, q_g, k_g)
    cos, sin = axial_rope_tables(h.shape[1])
    q = apply_rope(q, cos, sin)
    k = apply_rope(k, cos, sin)
    k_all = jnp.concatenate([k, k_ctx.astype(k.dtype)], axis=1)
    v_all = jnp.concatenate([v, v_ctx.astype(v.dtype)], axis=1)
    attn = block_attention(q, k_all, v_all)
    mix = jnp.concatenate([attn, fourier_mix(u, w_f)], axis=-1)
    return jnp.einsum('bse,ed->bsd', mix, w_o)


def modulation(cond, w_ada, b_ada):
    m = jnp.einsum('bd,de->be', jax.nn.silu(cond), w_ada) + b_ada
    return [t[:, None, :] for t in jnp.split(m, N_MOD, axis=-1)]


def moe(h, w_router, router_bias, w1, w3, w2):
    b, s, d = h.shape
    t = h.reshape(b * s, d)
    logits = jnp.einsum('td,de->te', t, w_router, preferred_element_type=jnp.float32)
    affinity = jax.nn.softmax(logits, axis=-1)
    sel = affinity + router_bias.astype(jnp.float32)
    grouped = sel.reshape(-1, N_EXPERT_GROUPS, EXPERTS_PER_GROUP)
    group_score = lax.top_k(grouped, TOP_K)[0].sum(axis=-1)
    best_group = jnp.argmax(group_score, axis=-1)
    expert_group = jnp.arange(N_EXPERTS) // EXPERTS_PER_GROUP
    masked = jnp.where(expert_group[None, :] == best_group[:, None], sel, -jnp.inf)
    _, top_idx = lax.top_k(masked, TOP_K)
    top_aff = jnp.take_along_axis(affinity, top_idx, axis=-1)
    top_w = top_aff / jnp.sum(top_aff, axis=-1, keepdims=True)
    gates = jnp.einsum('tk,tke->te', top_w, jax.nn.one_hot(top_idx, N_EXPERTS, dtype=jnp.float32))
    out = jnp.zeros(t.shape, jnp.float32)
    for e in range(N_EXPERTS):
        hid = jax.nn.silu(t @ w1[e]) * (t @ w3[e])
        out = out + gates[:, e:e + 1] * (hid @ w2[e]).astype(jnp.float32)
    return out.astype(h.dtype).reshape(b, s, d)


def setup_inputs(seed: int = 0) -> dict:
    key = jax.random.key(seed)
    ks = jax.random.split(key, 22)
    f32 = jnp.float32
    nrm = lambda k, shp, sc: jax.random.normal(k, shp, f32) * sc
    return {
        "x_prompt": nrm(ks[0], (BATCH, SEQ, D_MODEL), 1.0),
        "x_sample": nrm(ks[1], (DEC_BATCH, DEC_SEQ, D_MODEL), 1.0),
        "cache_k": nrm(ks[2], (DEC_BATCH, DEPTH, PAST_LEN, N_KV_HEADS, HEAD_DIM), 1.0),
        "cache_v": nrm(ks[3], (DEC_BATCH, DEPTH, PAST_LEN, N_KV_HEADS, HEAD_DIM), 1.0),
        "c": nrm(ks[4], (DEC_BATCH, D_MODEL), 1.0),
        "c_ctx": nrm(ks[5], (D_MODEL,), 1.0),
        "w_ada": nrm(ks[6], (DEPTH, D_MODEL, N_MOD * D_MODEL), 0.5 * D_MODEL ** -0.5),
        "b_ada": nrm(ks[7], (DEPTH, N_MOD * D_MODEL), 0.01),
        "norm_mix": 1.0 + nrm(ks[8], (DEPTH, D_MODEL), 0.02),
        "norm_ffn": 1.0 + nrm(ks[9], (DEPTH, D_MODEL), 0.02),
        "w_in": nrm(ks[10], (DEPTH, D_MODEL, IN_WIDTH), D_MODEL ** -0.5),
        "q_norm": 1.0 + nrm(ks[11], (DEPTH, HEAD_DIM), 0.02),
        "k_norm": 1.0 + nrm(ks[12], (DEPTH, HEAD_DIM), 0.02),
        "w_fourier": nrm(ks[13], (DEPTH, N_FOURIER_GROUPS, FOURIER_GROUP_DIM, FOURIER_GROUP_DIM), FOURIER_GROUP_DIM ** -0.5),
        "w_out": nrm(ks[14], (DEPTH, MIX_WIDTH, D_MODEL), MIX_WIDTH ** -0.5),
        "w_router": nrm(ks[15], (D_MODEL, N_EXPERTS), D_MODEL ** -0.5),
        "router_bias": nrm(ks[16], (N_EXPERTS,), 0.01),
        "w1": nrm(ks[17], (DEPTH, N_EXPERTS, D_MODEL, D_EXPERT), D_MODEL ** -0.5),
        "w3": nrm(ks[18], (DEPTH, N_EXPERTS, D_MODEL, D_EXPERT), D_MODEL ** -0.5),
        "w2": nrm(ks[19], (DEPTH, N_EXPERTS, D_EXPERT, D_MODEL), D_EXPERT ** -0.5),
    }


def reference(x_prompt, x_sample, cache_k, cache_v, c, c_ctx, w_ada, b_ada, norm_mix, norm_ffn,
              w_in, q_norm, k_norm, w_fourier, w_out, w_router, router_bias, w1, w3, w2):
    xp = x_prompt
    new_k, new_v = [], []
    for l in range(DEPTH):
        sh1, sc1, g1, sh2, sc2, g2 = modulation(c_ctx[None, :], w_ada[l], b_ada[l])
        h = rms_norm(xp, norm_mix[l]) * (1.0 + sc1) + sh1
        mix, k_l, v_l = context_mixer(h, w_in[l], q_norm[l], k_norm[l], w_fourier[l], w_out[l])
        xp = xp + g1 * mix
        h = rms_norm(xp, norm_ffn[l]) * (1.0 + sc2) + sh2
        xp = xp + g2 * moe(h, w_router, router_bias, w1[l], w3[l], w2[l])
        new_k.append(k_l)
        new_v.append(v_l)
    new_cache_k = jnp.stack(new_k, axis=1)
    new_cache_v = jnp.stack(new_v, axis=1)

    xs = x_sample
    for l in range(DEPTH):
        sh1, sc1, g1, sh2, sc2, g2 = modulation(c, w_ada[l], b_ada[l])
        h = rms_norm(xs, norm_mix[l]) * (1.0 + sc1) + sh1
        mix = latent_mixer(h, cache_k[:, l], cache_v[:, l], w_in[l], q_norm[l], k_norm[l],
                           w_fourier[l], w_out[l])
        xs = xs + g1 * mix
        h = rms_norm(xs, norm_ffn[l]) * (1.0 + sc2) + sh2
        xs = xs + g2 * moe(h, w_router, router_bias, w1[l], w3[l], w2[l])

    return (xp, xs, new_cache_k, new_cache_v)
```

```python
import functools
import math

import jax
import jax.numpy as jnp
from jax import lax
from jax.experimental import pallas as pl
from jax.experimental.pallas import tpu as pltpu

GRID_W = 64
N_EXPERT_GROUPS = 4
ROPE_THETA = 10000.0
EPS = 1e-6
N_MOD = 6
COND_ROWS = 8

VMEM_LIMIT_BYTES = 56 * 1024 * 1024
LANE = 128

BF16 = jnp.bfloat16
F32 = jnp.float32
U32 = jnp.uint32

_NT = (((1,), (1,)), ((), ()))


def _cp(*sem):
    return pltpu.CompilerParams(dimension_semantics=sem, vmem_limit_bytes=VMEM_LIMIT_BYTES)


def _tile(n, pref):
    t = min(n, pref)
    while n % t:
        t -= 1
    return t


def _pack_pair(a, b):
    ua = lax.bitcast_convert_type(a.astype(BF16).astype(F32), U32)
    ub = lax.bitcast_convert_type(b.astype(BF16).astype(F32), U32)
    return (ua >> 16) | (ub & jnp.uint32(0xFFFF0000))


def _unpack_pair(w):
    lo = lax.bitcast_convert_type(w << 16, F32)
    hi = lax.bitcast_convert_type(w & jnp.uint32(0xFFFF0000), F32)
    return lo, hi


def _mod_kernel(c_ref, w_ref, b_ref, o_ref):
    c = c_ref[...]
    s = (c * jax.nn.sigmoid(c)).astype(BF16)
    o_ref[...] = jnp.dot(s, w_ref[...].astype(BF16), preferred_element_type=F32) + b_ref[...]


def _modulation(cond, w_ada, b_ada):
    depth, d, n = w_ada.shape
    tn = _tile(n, 512)
    return pl.pallas_call(
        _mod_kernel,
        grid=(depth, n // tn),
        in_specs=[
            pl.BlockSpec((COND_ROWS, d), lambda l, j: (0, 0)),
            pl.BlockSpec((None, d, tn), lambda l, j: (l, 0, j)),
            pl.BlockSpec((None, 1, tn), lambda l, j: (l, 0, j)),
        ],
        out_specs=pl.BlockSpec((None, COND_ROWS, tn), lambda l, j: (l, 0, j)),
        out_shape=jax.ShapeDtypeStruct((depth, COND_ROWS, n), F32),
        compiler_params=_cp("arbitrary", "arbitrary"),
        name="modulation",
    )(cond, w_ada, b_ada.reshape(depth, 1, n))


class _Rows:
    def __init__(self, t_ctx, dec_seq, tm):
        assert t_ctx % tm == 0 and dec_seq % tm == 0
        self.tm = tm
        self.n_ctx_tiles = t_ctx // tm
        self.tiles_per_seq = dec_seq // tm

    def cond(self, i):
        lat = 1 + (i - self.n_ctx_tiles) // self.tiles_per_seq
        return jnp.where(i < self.n_ctx_tiles, 0, lat)

    def seq_tile(self, i):
        return jnp.maximum(i - self.n_ctx_tiles, 0) % self.tiles_per_seq


def _mod_spec(rows, layer, which, width, col=None):
    if col is None:
        return pl.BlockSpec((None, None, None, 1, width),
                            lambda i, *_: (layer, rows.cond(i), which, 0, 0))
    return pl.BlockSpec((None, None, None, 1, width),
                        lambda i, j, *_: (layer, rows.cond(i), which, 0, col(j)))


def _norm_mod(x, g, sc, sh):
    ms = jnp.mean(x * x, axis=-1, keepdims=True)
    y = x * lax.rsqrt(ms + EPS) * g
    return y * (1.0 + sc) + sh


def _norm_mod_kernel(x_ref, g_ref, sc_ref, sh_ref, o_ref):
    o_ref[...] = _norm_mod(x_ref[...], g_ref[...], sc_ref[...], sh_ref[...]).astype(o_ref.dtype)


def _norm_modulate(x, gain, mods, layer, t_ctx, dec_seq):
    t, d = x.shape
    rows = _Rows(t_ctx, dec_seq, _tile(min(t_ctx, dec_seq), 256))
    tm = rows.tm
    return pl.pallas_call(
        _norm_mod_kernel,
        grid=(t // tm,),
        in_specs=[
            pl.BlockSpec((tm, d), lambda i: (i, 0)),
            pl.BlockSpec((None, 1, d), lambda i: (layer, 0, 0)),
            _mod_spec(rows, layer, 1, d),
            _mod_spec(rows, layer, 0, d),
        ],
        out_specs=pl.BlockSpec((tm, d), lambda i: (i, 0)),
        out_shape=jax.ShapeDtypeStruct((t, d), BF16),
        compiler_params=_cp("arbitrary"),
        name="norm_modulate",
    )(x, gain, mods, mods)


def _proj_kernel(h_ref, w_ref, qg_ref, kg_ref, cos_ref, se_ref, so_ref,
                 q_ref, k_ref, v_ref, u_ref, *, nq, nk, nv, n_ctx_tiles, hd, q_scale):
    i = pl.program_id(0)
    j = pl.program_id(1)
    acc = jnp.dot(h_ref[...], w_ref[...].astype(BF16), preferred_element_type=F32)
    heads = acc.shape[1] // hd

    def head_norm(gain_ref, out_ref, mult, rope):
        for hh in range(heads):
            sl = slice(hh * hd, (hh + 1) * hd)
            blk = acc[:, sl]
            ms = jnp.mean(blk * blk, axis=-1, keepdims=True)
            y = blk * lax.rsqrt(ms + EPS) * gain_ref[...]
            if rope:
                y = (y * cos_ref[...] + pltpu.roll(y, hd - 1, 1) * se_ref[...]
                     + pltpu.roll(y, 1, 1) * so_ref[...])
            if mult != 1.0:
                y = y * mult
            out_ref[:, sl] = y.astype(out_ref.dtype)

    def qk_path(gain_ref, out_ref, mult):
        @pl.when(i < n_ctx_tiles)
        def _():
            head_norm(gain_ref, out_ref, mult, False)

        @pl.when(i >= n_ctx_tiles)
        def _():
            head_norm(gain_ref, out_ref, mult, True)

    @pl.when(j < nq)
    def _():
        qk_path(qg_ref, q_ref, q_scale)

    @pl.when(jnp.logical_and(j >= nq, j < nq + nk))
    def _():
        qk_path(kg_ref, k_ref, 1.0)

    @pl.when(jnp.logical_and(j >= nq + nk, j < nq + nk + nv))
    def _():
        v_ref[...] = acc

    @pl.when(j >= nq + nk + nv)
    def _():
        u_ref[...] = acc.astype(u_ref.dtype)


def _rope_tables(n_tok, hd):
    axis_dim = hd // 2
    rows = n_tok // GRID_W
    row_idx = jnp.repeat(jnp.arange(rows, dtype=F32), GRID_W)
    col_idx = jnp.tile(jnp.arange(GRID_W, dtype=F32), rows)
    freqs = ROPE_THETA ** (-jnp.arange(0, axis_dim, 2, dtype=F32) / axis_dim)
    ang = jnp.concatenate([row_idx[:, None] * freqs, col_idx[:, None] * freqs], axis=-1)
    cos = jnp.repeat(jnp.cos(ang), 2, axis=-1)
    sin = jnp.repeat(jnp.sin(ang), 2, axis=-1)
    even = (jnp.arange(hd) % 2 == 0)[None, :]
    return cos, jnp.where(even, -sin, 0.0), jnp.where(even, 0.0, sin)


def _project(h, w_in, q_g, k_g, rope, layer, dims):
    t, d = h.shape
    hd, aw, kvw, fw = dims["hd"], dims["aw"], dims["kvw"], dims["fw"]
    tn = _tile(math.gcd(math.gcd(aw, kvw), fw), 512)
    rows = _Rows(dims["t_ctx"], dims["dec_seq"], _tile(min(dims["t_ctx"], dims["dec_seq"]), 1024))
    tm = rows.tm
    nq, nk, nv, nu = aw // tn, kvw // tn, kvw // tn, fw // tn
    cos, se, so = rope
    kern = functools.partial(_proj_kernel, nq=nq, nk=nk, nv=nv, n_ctx_tiles=rows.n_ctx_tiles,
                             hd=hd, q_scale=float(hd) ** -0.5)

    def clampj(lo, n):
        return lambda i, j: (i, jnp.clip(j - lo, 0, n - 1))

    rope_spec = pl.BlockSpec((tm, hd), lambda i, j: (rows.seq_tile(i), 0))
    return pl.pallas_call(
        kern,
        grid=(t // tm, nq + nk + nv + nu),
        in_specs=[
            pl.BlockSpec((tm, d), lambda i, j: (i, 0)),
            pl.BlockSpec((None, d, tn), lambda i, j: (layer, 0, j)),
            pl.BlockSpec((None, 1, hd), lambda i, j: (layer, 0, 0)),
            pl.BlockSpec((None, 1, hd), lambda i, j: (layer, 0, 0)),
            rope_spec, rope_spec, rope_spec,
        ],
        out_specs=[
            pl.BlockSpec((tm, tn), clampj(0, nq)),
            pl.BlockSpec((tm, tn), clampj(nq, nk)),
            pl.BlockSpec((tm, tn), clampj(nq + nk, nv)),
            pl.BlockSpec((tm, tn), clampj(nq + nk + nv, nu)),
        ],
        out_shape=[
            jax.ShapeDtypeStruct((t, aw), BF16),
            jax.ShapeDtypeStruct((t, kvw), F32),
            jax.ShapeDtypeStruct((t, kvw), F32),
            jax.ShapeDtypeStruct((t, fw), BF16),
        ],
        compiler_params=_cp("arbitrary", "arbitrary"),
        name="in_projection",
    )(h, w_in, q_g, k_g, cos, se, so)


def _softmax_pv(score_blocks, value_blocks):
    m = score_blocks[0].max(axis=-1, keepdims=True)
    for s in score_blocks[1:]:
        m = jnp.maximum(m, s.max(axis=-1, keepdims=True))
    den = None
    out = None
    for s, v in zip(score_blocks, value_blocks):
        p = jnp.exp(s - m)
        ps = p.sum(axis=-1, keepdims=True)
        pv = jnp.dot(p.astype(BF16), v, preferred_element_type=F32)
        den = ps if den is None else den + ps
        out = pv if out is None else out + pv
    return out * (1.0 / den)


def _attn_ctx_kernel(q_ref, k_ref, v_ref, o_ref, *, nkv, qpk, hd):
    s_len = q_ref.shape[0]
    for kv in range(nkv):
        kb = k_ref[:, kv * hd:(kv + 1) * hd].astype(BF16)
        vb = v_ref[:, kv * hd:(kv + 1) * hd].astype(BF16)
        q3 = jnp.concatenate(
            [q_ref[:, (kv * qpk + g) * hd:(kv * qpk + g + 1) * hd] for g in range(qpk)], axis=0)
        sc = lax.dot_general(q3, kb, _NT, preferred_element_type=F32)
        o = _softmax_pv([sc], [vb])
        for g in range(qpk):
            o_ref[:, (kv * qpk + g) * hd:(kv * qpk + g + 1) * hd] = (
                o[g * s_len:(g + 1) * s_len].astype(o_ref.dtype))


def _attn_lat_kernel(q_ref, k_ref, v_ref, ck_ref, cv_ref, o_ref, *, qpk, hd):
    tq = q_ref.shape[0]
    q3 = jnp.concatenate([q_ref[:, g * hd:(g + 1) * hd] for g in range(qpk)], axis=0)
    kb = k_ref[...].astype(BF16)
    ckb = ck_ref[...].astype(BF16)
    s1 = lax.dot_general(q3, kb, _NT, preferred_element_type=F32)
    s2 = lax.dot_general(q3, ckb, _NT, preferred_element_type=F32)
    o = _softmax_pv([s1, s2], [v_ref[...].astype(BF16), cv_ref[...].astype(BF16)])
    for g in range(qpk):
        o_ref[:, g * hd:(g + 1) * hd] = o[g * tq:(g + 1) * tq].astype(o_ref.dtype)


def _attention(q, k, v, cache_k4, cache_v4, layer, dims):
    t, aw = q.shape
    hd, nkv, qpk = dims["hd"], dims["nkv"], dims["qpk"]
    seq, dec_seq, t_ctx, past = dims["seq"], dims["dec_seq"], dims["t_ctx"], dims["past"]
    n_ctx = t_ctx // seq
    kvw = nkv * hd
    ctx = pl.pallas_call(
        functools.partial(_attn_ctx_kernel, nkv=nkv, qpk=qpk, hd=hd),
        grid=(n_ctx,),
        in_specs=[
            pl.BlockSpec((seq, aw), lambda b: (b, 0)),
            pl.BlockSpec((seq, kvw), lambda b: (b, 0)),
            pl.BlockSpec((seq, kvw), lambda b: (b, 0)),
        ],
        out_specs=pl.BlockSpec((seq, aw), lambda b: (b, 0)),
        out_shape=jax.ShapeDtypeStruct((t_ctx, aw), BF16),
        compiler_params=_cp("arbitrary"),
        name="attention_context",
    )(q, k, v)

    assert t_ctx % dec_seq == 0
    n_lat = (t - t_ctx) // dec_seq
    tq = _tile(dec_seq, 512)
    nqb = dec_seq // tq
    q_off = t_ctx // tq
    k_off = t_ctx // dec_seq
    qw = qpk * hd
    lat = pl.pallas_call(
        functools.partial(_attn_lat_kernel, qpk=qpk, hd=hd),
        grid=(n_lat, nkv, nqb),
        in_specs=[
            pl.BlockSpec((tq, qw), lambda b, kv, qi: (q_off + b * nqb + qi, kv)),
            pl.BlockSpec((dec_seq, hd), lambda b, kv, qi: (k_off + b, kv)),
            pl.BlockSpec((dec_seq, hd), lambda b, kv, qi: (k_off + b, kv)),
            pl.BlockSpec((None, None, past, hd), lambda b, kv, qi: (b, layer, 0, kv)),
            pl.BlockSpec((None, None, past, hd), lambda b, kv, qi: (b, layer, 0, kv)),
        ],
        out_specs=pl.BlockSpec((tq, qw), lambda b, kv, qi: (b * nqb + qi, kv)),
        out_shape=jax.ShapeDtypeStruct((t - t_ctx, aw), BF16),
        compiler_params=_cp("arbitrary", "arbitrary", "arbitrary"),
        name="attention_latent",
    )(q, k, v, cache_k4, cache_v4)
    return ctx, lat


def _fourier_kernel(u_ref, dpos_ref, dch_ref, wf_ref, o_ref, tt_ref, *, ng, gd, norm):
    s_len = u_ref.shape[0]
    for g in range(ng):
        sl = slice(g * gd, (g + 1) * gd)
        t = jnp.dot(u_ref[:, sl], dch_ref[...], preferred_element_type=F32)
        tt_ref[0:s_len, sl] = t[:, :gd].astype(BF16)
        tt_ref[s_len:2 * s_len, sl] = t[:, gd:].astype(BF16)
    f = (jnp.dot(dpos_ref[...], tt_ref[...], preferred_element_type=F32) * norm).astype(BF16)
    for g in range(ng):
        sl = slice(g * gd, (g + 1) * gd)
        o_ref[:, sl] = jnp.dot(f[:, sl], wf_ref[g].astype(BF16),
                               preferred_element_type=F32).astype(o_ref.dtype)


def _dft_tables(n):
    jk = (jnp.arange(n, dtype=jnp.int32)[:, None] * jnp.arange(n, dtype=jnp.int32)[None, :]) % n
    ang = jk.astype(F32) * (2.0 * jnp.pi / n)
    return jnp.cos(ang), jnp.sin(ang)


def _fourier(u, w_fourier, layer, s_len, row_off, n_seq, tables):
    ng, gd = w_fourier.shape[1], w_fourier.shape[2]
    fw = ng * gd
    dpos, dch = tables
    off = row_off // s_len
    return pl.pallas_call(
        functools.partial(_fourier_kernel, ng=ng, gd=gd, norm=float(s_len * gd) ** -0.5),
        grid=(n_seq,),
        in_specs=[
            pl.BlockSpec((s_len, fw), lambda b: (off + b, 0)),
            pl.BlockSpec((s_len, 2 * s_len), lambda b: (0, 0)),
            pl.BlockSpec((gd, 2 * gd), lambda b: (0, 0)),
            pl.BlockSpec((None, ng, gd, gd), lambda b: (layer, 0, 0, 0)),
        ],
        out_specs=pl.BlockSpec((s_len, fw), lambda b: (b, 0)),
        out_shape=jax.ShapeDtypeStruct((n_seq * s_len, fw), BF16),
        scratch_shapes=[pltpu.VMEM((2 * s_len, fw), BF16)],
        compiler_params=_cp("arbitrary"),
        name="fourier_mix",
    )(u, dpos, dch, w_fourier)


def _wout_kernel(a_ref, f_ref, w_ref, x_ref, g_ref, o_ref, *, aw):
    acc = jnp.dot(a_ref[...], w_ref[:aw, :].astype(BF16), preferred_element_type=F32)
    acc = acc + jnp.dot(f_ref[...], w_ref[aw:, :].astype(BF16), preferred_element_type=F32)
    o_ref[...] = x_ref[...] + g_ref[...] * acc


def _out_project(attn, four, w_out, x, mods, layer, dims):
    t, d = x.shape
    aw, fw = attn.shape[1], four.shape[1]
    rows = _Rows(dims["t_ctx"], dims["dec_seq"], _tile(min(dims["t_ctx"], dims["dec_seq"]), 1024))
    tm = rows.tm
    tn = _tile(d, 512)
    return pl.pallas_call(
        functools.partial(_wout_kernel, aw=aw),
        grid=(t // tm, d // tn),
        in_specs=[
            pl.BlockSpec((tm, aw), lambda i, j: (i, 0)),
            pl.BlockSpec((tm, fw), lambda i, j: (i, 0)),
            pl.BlockSpec((None, aw + fw, tn), lambda i, j: (layer, 0, j)),
            pl.BlockSpec((tm, tn), lambda i, j: (i, j)),
            _mod_spec(rows, layer, 2, tn, col=lambda j: j),
        ],
        out_specs=pl.BlockSpec((tm, tn), lambda i, j: (i, j)),
        out_shape=jax.ShapeDtypeStruct((t, d), F32),
        compiler_params=_cp("arbitrary", "arbitrary"),
        name="out_projection",
    )(attn, four, w_out, x, mods)


def _top2_sum(a, b, c, d):
    hi1, lo1 = jnp.maximum(a, b), jnp.minimum(a, b)
    hi2, lo2 = jnp.maximum(c, d), jnp.minimum(c, d)
    return jnp.maximum(hi1, hi2) + jnp.maximum(jnp.minimum(hi1, hi2), jnp.maximum(lo1, lo2))


def _norm_router_kernel(x_ref, g_ref, sc_ref, sh_ref, wr_ref, rb_ref, hp_ref, idx_ref, wt_ref,
                        *, n_exp, n_grp):
    hf = _norm_mod(x_ref[...], g_ref[...], sc_ref[...], sh_ref[...])
    half = hf.shape[1] // 2
    hp_ref[...] = _pack_pair(hf[:, :half], hf[:, half:])

    hi = hf.astype(BF16)
    lo = (hf - hi.astype(F32)).astype(BF16)
    w = wr_ref[...]
    whi = w.astype(BF16)
    wlo = (w - whi.astype(F32)).astype(BF16)
    lg = (lax.dot_general(whi, hi, _NT, preferred_element_type=F32)
          + lax.dot_general(whi, lo, _NT, preferred_element_type=F32)
          + lax.dot_general(wlo, hi, _NT, preferred_element_type=F32))
    ex = jnp.exp(lg - lg.max(axis=0, keepdims=True))
    aff = ex * (1.0 / ex.sum(axis=0, keepdims=True))
    sel = aff + rb_ref[...]

    per = n_exp // n_grp
    row = lambda a, e: a[e:e + 1, :]
    assert per == 4, "group score uses a 4-element top-2 network"
    best_s = _top2_sum(*[row(sel, e) for e in range(per)])
    best_g = jnp.zeros_like(best_s, dtype=jnp.int32)
    for g in range(1, n_grp):
        s = _top2_sum(*[row(sel, g * per + e) for e in range(per)])
        upd = s > best_s
        best_g = jnp.where(upd, g, best_g)
        best_s = jnp.where(upd, s, best_s)

    neg = jnp.float32(-jnp.inf)
    picks = []
    for _ in range(2):
        bv = jnp.full_like(best_s, neg)
        bi = jnp.zeros_like(best_g)
        ba = jnp.zeros_like(best_s)
        for e in range(n_exp):
            ok = best_g == (e // per)
            for (pi, _) in picks:
                ok = jnp.logical_and(ok, pi != e)
            val = jnp.where(ok, row(sel, e), neg)
            upd = val > bv
            bv = jnp.where(upd, val, bv)
            bi = jnp.where(upd, e, bi)
            ba = jnp.where(upd, row(aff, e), ba)
        picks.append((bi, ba))
    (i0, a0), (i1, a1) = picks
    inv = 1.0 / (a0 + a1)
    idx_ref[0:1, :] = i0
    idx_ref[1:2, :] = i1
    wt_ref[0:1, :] = a0 * inv
    wt_ref[1:2, :] = a1 * inv


def _norm_router(x, gain, mods, w_router_t, router_bias, layer, dims):
    t, d = x.shape
    n_exp = w_router_t.shape[0]
    rows = _Rows(dims["t_ctx"], dims["dec_seq"], _tile(min(dims["t_ctx"], dims["dec_seq"]), 256))
    tm = rows.tm
    return pl.pallas_call(
        functools.partial(_norm_router_kernel, n_exp=n_exp, n_grp=N_EXPERT_GROUPS),
        grid=(t // tm,),
        in_specs=[
            pl.BlockSpec((tm, d), lambda i: (i, 0)),
            pl.BlockSpec((None, 1, d), lambda i: (layer, 0, 0)),
            _mod_spec(rows, layer, 4, d),
            _mod_spec(rows, layer, 3, d),
            pl.BlockSpec((n_exp, d), lambda i: (0, 0)),
            pl.BlockSpec((n_exp, 1), lambda i: (0, 0)),
        ],
        out_specs=[
            pl.BlockSpec((tm, d // 2), lambda i: (i, 0)),
            pl.BlockSpec((2, tm), lambda i: (0, i)),
            pl.BlockSpec((2, tm), lambda i: (0, i)),
        ],
        out_shape=[
            jax.ShapeDtypeStruct((t, d // 2), U32),
            jax.ShapeDtypeStruct((2, t), jnp.int32),
            jax.ShapeDtypeStruct((2, t), F32),
        ],
        compiler_params=_cp("arbitrary"),
        name="norm_router",
    )(x, gain, mods, mods, w_router_t, router_bias.reshape(n_exp, 1))


def _dispatch_plan(idx, n_exp, tile_rows, n_tiles):
    t = idx.shape[1]
    e_a = idx.reshape(-1)
    onehot = (e_a[:, None] == jnp.arange(n_exp, dtype=jnp.int32)[None, :]).astype(jnp.int32)
    csum = jnp.cumsum(onehot, axis=0)
    counts = csum[-1]
    rank = jnp.take_along_axis(csum, e_a[:, None], axis=1)[:, 0] - 1
    padded = ((counts + tile_rows - 1) // tile_rows) * tile_rows
    ends = jnp.cumsum(padded)
    pos = (ends - padded)[e_a] + rank
    tok = jnp.tile(jnp.arange(t, dtype=jnp.int32), 2)
    slot_tok = jnp.zeros((n_tiles * tile_rows,), jnp.int32).at[pos].set(tok)
    tile_start = jnp.arange(n_tiles, dtype=jnp.int32) * tile_rows
    tile_exp = jnp.minimum(jnp.searchsorted(ends, tile_start, side="right"), n_exp - 1)
    n_valid = ends[-1] // tile_rows
    return slot_tok, pos.reshape(2, t), tile_exp.astype(jnp.int32), n_valid.astype(jnp.int32).reshape(1)


def _row_copy(src_hbm, dst, sem, src_row, dst_row, n=1):
    return pltpu.make_async_copy(src_hbm.at[pl.ds(src_row, n)], dst.at[pl.ds(dst_row, n)], sem)


def _gather_kernel(tok_ref, src_ref, dst_ref, sem, *, n):
    base = pl.program_id(0) * n

    def issue(r, carry):
        _row_copy(src_ref, dst_ref, sem, tok_ref[0, 0, r], base + r).start()
        return carry

    lax.fori_loop(0, n, issue, 0)
    _row_copy(src_ref, dst_ref, sem, 0, base, n).wait()


def _gather_rows(src, slot_tok, chunk):
    r_total = slot_tok.shape[0]
    n_chunks = r_total // chunk
    return pl.pallas_call(
        functools.partial(_gather_kernel, n=chunk),
        grid=(n_chunks,),
        in_specs=[
            pl.BlockSpec((1, 1, chunk), lambda i: (i, 0, 0), memory_space=pltpu.SMEM),
            pl.BlockSpec(memory_space=pl.ANY),
        ],
        out_specs=pl.BlockSpec(memory_space=pl.ANY),
        out_shape=jax.ShapeDtypeStruct((r_total, src.shape[1]), src.dtype),
        scratch_shapes=[pltpu.SemaphoreType.DMA],
        compiler_params=pltpu.CompilerParams(dimension_semantics=("arbitrary",),
                                             has_side_effects=True),
        name="dispatch_gather",
    )(slot_tok.reshape(n_chunks, 1, chunk), src)


def _ffn_up_kernel(te_ref, nv_ref, x_ref, w1_ref, w3_ref, o_ref):
    @pl.when(pl.program_id(0) < nv_ref[0])
    def _():
        xl, xr = _unpack_pair(x_ref[...])
        xl = xl.astype(BF16)
        xr = xr.astype(BF16)
        half = xl.shape[1]

        def up(w_ref):
            return (jnp.dot(xl, w_ref[:half, :].astype(BF16), preferred_element_type=F32)
                    + jnp.dot(xr, w_ref[half:, :].astype(BF16), preferred_element_type=F32))

        a = up(w1_ref)
        b = up(w3_ref)
        o_ref[...] = (a * jax.nn.sigmoid(a) * b).astype(o_ref.dtype)

    @pl.when(pl.program_id(0) >= nv_ref[0])
    def _():
        o_ref[...] = jnp.zeros_like(o_ref)


def _ffn_down_kernel(te_ref, nv_ref, h_ref, wl_ref, wr_ref, o_ref):
    @pl.when(pl.program_id(0) < nv_ref[0])
    def _():
        h = h_ref[...]
        yl = jnp.dot(h, wl_ref[...].astype(BF16), preferred_element_type=F32)
        yr = jnp.dot(h, wr_ref[...].astype(BF16), preferred_element_type=F32)
        o_ref[...] = _pack_pair(yl, yr)

    @pl.when(pl.program_id(0) >= nv_ref[0])
    def _():
        o_ref[...] = jnp.zeros_like(o_ref)


def _expert_ffn(xs, tile_exp, n_valid, w1, w3, w2, layer, tile_rows):
    r_total, dh = xs.shape
    d = 2 * dh
    f = w1.shape[3]
    n_tiles = r_total // tile_rows
    tf = _tile(f, 256)
    nf = f // tf

    def live(w, nv):
        return jnp.minimum(w, nv[0] - 1)

    def chunk(w, c, nv, n_chunks):
        return jnp.where(w < nv[0], c, n_chunks - 1)

    hid = pl.pallas_call(
        _ffn_up_kernel,
        grid_spec=pltpu.PrefetchScalarGridSpec(
            num_scalar_prefetch=2,
            grid=(n_tiles, nf),
            in_specs=[
                pl.BlockSpec((tile_rows, dh), lambda w, c, te, nv: (live(w, nv), 0)),
                pl.BlockSpec((None, None, d, tf),
                             lambda w, c, te, nv: (layer, te[live(w, nv)], 0, chunk(w, c, nv, nf))),
                pl.BlockSpec((None, None, d, tf),
                             lambda w, c, te, nv: (layer, te[live(w, nv)], 0, chunk(w, c, nv, nf))),
            ],
            out_specs=pl.BlockSpec((tile_rows, tf),
                                   lambda w, c, te, nv: (w, c)),
        ),
        out_shape=jax.ShapeDtypeStruct((r_total, f), BF16),
        compiler_params=_cp("arbitrary", "arbitrary"),
        name="expert_ffn_up",
    )(tile_exp, n_valid, xs, w1, w3)

    tn = _tile(dh, 512)
    nn = dh // tn
    return pl.pallas_call(
        _ffn_down_kernel,
        grid_spec=pltpu.PrefetchScalarGridSpec(
            num_scalar_prefetch=2,
            grid=(n_tiles, nn),
            in_specs=[
                pl.BlockSpec((tile_rows, f), lambda w, c, te, nv: (live(w, nv), 0)),
                pl.BlockSpec((None, None, f, tn),
                             lambda w, c, te, nv: (layer, te[live(w, nv)], 0, chunk(w, c, nv, nn))),
                pl.BlockSpec((None, None, f, tn),
                             lambda w, c, te, nv: (layer, te[live(w, nv)], 0,
                                                   nn + chunk(w, c, nv, nn))),
            ],
            out_specs=pl.BlockSpec((tile_rows, tn),
                                   lambda w, c, te, nv: (w, c)),
        ),
        out_shape=jax.ShapeDtypeStruct((r_total, dh), U32),
        compiler_params=_cp("arbitrary", "arbitrary"),
        name="expert_ffn_down",
    )(tile_exp, n_valid, hid, w2, w2)


def _combine_kernel(pos_ref, y_ref, x_ref, g_ref, wt_ref, o_ref, ybuf, sem, *, tm):
    def issue(r, carry):
        _row_copy(y_ref, ybuf, sem, pos_ref[0, 0, r], r).start()
        return carry

    lax.fori_loop(0, 2 * tm, issue, 0)
    _row_copy(y_ref, ybuf, sem, 0, 0, 2 * tm).wait()

    half = ybuf.shape[1]
    w0 = wt_ref[:, 0:1]
    w1 = wt_ref[:, 1:2]
    l0, r0 = _unpack_pair(ybuf[0:tm, :])
    l1, r1 = _unpack_pair(ybuf[tm:2 * tm, :])
    o_ref[:, :half] = x_ref[:, :half] + g_ref[:, :half] * (w0 * l0 + w1 * l1)
    o_ref[:, half:] = x_ref[:, half:] + g_ref[:, half:] * (w0 * r0 + w1 * r1)


def _combine(y, pos, wts, x, mods, layer, dims):
    t, d = x.shape
    rows = _Rows(dims["t_ctx"], dims["dec_seq"], _tile(min(dims["t_ctx"], dims["dec_seq"]), 256))
    tm = rows.tm
    nt = t // tm
    pos_tiles = pos.reshape(2, nt, tm).transpose(1, 0, 2).reshape(nt, 1, 2 * tm)
    return pl.pallas_call(
        functools.partial(_combine_kernel, tm=tm),
        grid=(nt,),
        in_specs=[
            pl.BlockSpec((1, 1, 2 * tm), lambda i: (i, 0, 0), memory_space=pltpu.SMEM),
            pl.BlockSpec(memory_space=pl.ANY),
            pl.BlockSpec((tm, d), lambda i: (i, 0)),
            _mod_spec(rows, layer, 5, d),
            pl.BlockSpec((tm, 2), lambda i: (i, 0)),
        ],
        out_specs=pl.BlockSpec((tm, d), lambda i: (i, 0)),
        out_shape=jax.ShapeDtypeStruct((t, d), F32),
        scratch_shapes=[pltpu.VMEM((2 * tm, d // 2), U32), pltpu.SemaphoreType.DMA],
        compiler_params=_cp("arbitrary"),
        name="moe_combine",
    )(pos_tiles, y, x, mods, wts.T)


MOE_TILE_ROWS = 512
GATHER_CHUNK = 512


def kernel(x_prompt, x_sample, cache_k, cache_v, c, c_ctx, w_ada, b_ada, norm_mix, norm_ffn, w_in, q_norm, k_norm, w_fourier, w_out, w_router, router_bias, w1, w3, w2):
    batch, seq, d = x_prompt.shape
    dec_batch, dec_seq, _ = x_sample.shape
    depth = w_in.shape[0]
    past, nkv, hd = cache_k.shape[2], cache_k.shape[3], cache_k.shape[4]
    ng, gd = w_fourier.shape[1], w_fourier.shape[2]
    fw = ng * gd
    kvw = nkv * hd
    aw = w_out.shape[1] - fw
    n_exp = w_router.shape[1]
    t_ctx = batch * seq
    t_lat = dec_batch * dec_seq
    t = t_ctx + t_lat
    assert hd == LANE and dec_batch + 1 <= COND_ROWS and w_in.shape[2] == aw + 2 * kvw + fw
    dims = dict(hd=hd, aw=aw, kvw=kvw, fw=fw, nkv=nkv, qpk=aw // kvw, seq=seq, dec_seq=dec_seq,
                t_ctx=t_ctx, past=past)

    cond = jnp.zeros((COND_ROWS, d), F32).at[0].set(c_ctx).at[1:1 + dec_batch].set(c)
    mods = _modulation(cond, w_ada, b_ada).reshape(depth, COND_ROWS, N_MOD, 1, d)

    rope = _rope_tables(dec_seq, hd)
    cs_c, sn_c = _dft_tables(seq)
    cs_l, sn_l = _dft_tables(dec_seq)
    cs_g, sn_g = _dft_tables(gd)
    dch = jnp.concatenate([cs_g, sn_g], axis=1).astype(BF16)
    dft_ctx = (jnp.concatenate([cs_c, -sn_c], axis=1).astype(BF16), dch)
    dft_lat = (jnp.concatenate([cs_l, -sn_l], axis=1).astype(BF16), dch)

    cache_k4 = cache_k.reshape(dec_batch, depth, past, kvw)
    cache_v4 = cache_v.reshape(dec_batch, depth, past, kvw)
    w_router_t = w_router.T
    n_moe_tiles = (2 * t) // MOE_TILE_ROWS + n_exp
    norm_mix3 = norm_mix.reshape(depth, 1, d)
    norm_ffn3 = norm_ffn.reshape(depth, 1, d)
    q_norm3 = q_norm.reshape(depth, 1, hd)
    k_norm3 = k_norm.reshape(depth, 1, hd)

    x = jnp.concatenate([x_prompt.reshape(t_ctx, d), x_sample.reshape(t_lat, d)], axis=0)
    new_k, new_v = [], []
    for l in range(depth):
        h = _norm_modulate(x, norm_mix3, mods, l, t_ctx, dec_seq)
        q, k, v, u = _project(h, w_in, q_norm3, k_norm3, rope, l, dims)
        new_k.append(k[:t_ctx].reshape(batch, seq, nkv, hd))
        new_v.append(v[:t_ctx].reshape(batch, seq, nkv, hd))
        attn_c, attn_l = _attention(q, k, v, cache_k4, cache_v4, l, dims)
        four_c = _fourier(u, w_fourier, l, seq, 0, batch, dft_ctx)
        four_l = _fourier(u, w_fourier, l, dec_seq, t_ctx, dec_batch, dft_lat)
        attn = jnp.concatenate([attn_c, attn_l], axis=0)
        four = jnp.concatenate([four_c, four_l], axis=0)
        x = _out_project(attn, four, w_out, x, mods, l, dims)

        hp, idx, wts = _norm_router(x, norm_ffn3, mods, w_router_t, router_bias, l, dims)
        slot_tok, pos, tile_exp, n_valid = _dispatch_plan(idx, n_exp, MOE_TILE_ROWS, n_moe_tiles)
        xs = _gather_rows(hp, slot_tok, GATHER_CHUNK)
        y = _expert_ffn(xs, tile_exp, n_valid, w1, w3, w2, l, MOE_TILE_ROWS)
        x = _combine(y, pos, wts, x, mods, l, dims)

    y_prompt = x[:t_ctx].reshape(batch, seq, d)
    y_sample = x[t_ctx:].reshape(dec_batch, dec_seq, d)
    return (y_prompt, y_sample, jnp.stack(new_k, axis=1), jnp.stack(new_v, axis=1))
```

```python
import functools
import math

import jax
import jax.numpy as jnp
from jax import lax
from jax.experimental import pallas as pl
from jax.experimental.pallas import tpu as pltpu

GRID_W = 64
N_EXPERT_GROUPS = 4
ROPE_THETA = 10000.0
EPS = 1e-6
N_MOD = 6
COND_ROWS = 8

VMEM_LIMIT_BYTES = 56 * 1024 * 1024
LANE = 128

BF16 = jnp.bfloat16
F32 = jnp.float32
U32 = jnp.uint32

_NT = (((1,), (1,)), ((), ()))


def _cp(*sem):
    return pltpu.CompilerParams(dimension_semantics=sem, vmem_limit_bytes=VMEM_LIMIT_BYTES)


def _tile(n, pref):
    t = min(n, pref)
    while n % t:
        t -= 1
    return t


def _pack_pair(a, b):
    ua = lax.bitcast_convert_type(a.astype(BF16).astype(F32), U32)
    ub = lax.bitcast_convert_type(b.astype(BF16).astype(F32), U32)
    return (ua >> 16) | (ub & jnp.uint32(0xFFFF0000))


def _unpack_pair(w):
    lo = lax.bitcast_convert_type(w << 16, F32)
    hi = lax.bitcast_convert_type(w & jnp.uint32(0xFFFF0000), F32)
    return lo, hi


def _mod_kernel(c_ref, w_ref, b_ref, o_ref):
    c = c_ref[...]
    s = (c * jax.nn.sigmoid(c)).astype(BF16)
    o_ref[...] = jnp.dot(s, w_ref[...].astype(BF16), preferred_element_type=F32) + b_ref[...]


def _modulation(cond, w_ada, b_ada):
    depth, d, n = w_ada.shape
    tn = _tile(n, 512)
    return pl.pallas_call(
        _mod_kernel,
        grid=(depth, n // tn),
        in_specs=[
            pl.BlockSpec((COND_ROWS, d), lambda l, j: (0, 0)),
            pl.BlockSpec((None, d, tn), lambda l, j: (l, 0, j)),
            pl.BlockSpec((None, 1, tn), lambda l, j: (l, 0, j)),
        ],
        out_specs=pl.BlockSpec((None, COND_ROWS, tn), lambda l, j: (l, 0, j)),
        out_shape=jax.ShapeDtypeStruct((depth, COND_ROWS, n), F32),
        compiler_params=_cp("arbitrary", "arbitrary"),
        name="modulation",
    )(cond, w_ada, b_ada.reshape(depth, 1, n))


class _Rows:
    def __init__(self, t_ctx, dec_seq, tm):
        assert t_ctx % tm == 0 and dec_seq % tm == 0
        self.tm = tm
        self.n_ctx_tiles = t_ctx // tm
        self.tiles_per_seq = dec_seq // tm

    def cond(self, i):
        lat = 1 + (i - self.n_ctx_tiles) // self.tiles_per_seq
        return jnp.where(i < self.n_ctx_tiles, 0, lat)

    def seq_tile(self, i):
        return jnp.maximum(i - self.n_ctx_tiles, 0) % self.tiles_per_seq


def _mod_spec(rows, layer, which, width, col=None):
    if col is None:
        return pl.BlockSpec((None, None, None, 1, width),
                            lambda i, *_: (layer, rows.cond(i), which, 0, 0))
    return pl.BlockSpec((None, None, None, 1, width),
                        lambda i, j, *_: (layer, rows.cond(i), which, 0, col(j)))


def _norm_mod(x, g, sc, sh):
    ms = jnp.mean(x * x, axis=-1, keepdims=True)
    y = x * lax.rsqrt(ms + EPS) * g
    return y * (1.0 + sc) + sh


def _norm_mod_kernel(x_ref, g_ref, sc_ref, sh_ref, o_ref):
    o_ref[...] = _norm_mod(x_ref[...], g_ref[...], sc_ref[...], sh_ref[...]).astype(o_ref.dtype)


def _norm_modulate(x, gain, mods, layer, t_ctx, dec_seq):
    t, d = x.shape
    rows = _Rows(t_ctx, dec_seq, _tile(min(t_ctx, dec_seq), 256))
    tm = rows.tm
    return pl.pallas_call(
        _norm_mod_kernel,
        grid=(t // tm,),
        in_specs=[
            pl.BlockSpec((tm, d), lambda i: (i, 0)),
            pl.BlockSpec((None, 1, d), lambda i: (layer, 0, 0)),
            _mod_spec(rows, layer, 1, d),
            _mod_spec(rows, layer, 0, d),
        ],
        out_specs=pl.BlockSpec((tm, d), lambda i: (i, 0)),
        out_shape=jax.ShapeDtypeStruct((t, d), BF16),
        compiler_params=_cp("arbitrary"),
        name="norm_modulate",
    )(x, gain, mods, mods)


def _proj_kernel(h_ref, w_ref, qg_ref, kg_ref, cos_ref, se_ref, so_ref,
                 q_ref, k_ref, v_ref, u_ref, *, nq, nk, nv, n_ctx_tiles, hd, q_scale):
    i = pl.program_id(0)
    j = pl.program_id(1)
    acc = jnp.dot(h_ref[...], w_ref[...].astype(BF16), preferred_element_type=F32)
    heads = acc.shape[1] // hd

    def head_norm(gain_ref, out_ref, mult, rope):
        for hh in range(heads):
            sl = slice(hh * hd, (hh + 1) * hd)
            blk = acc[:, sl]
            ms = jnp.mean(blk * blk, axis=-1, keepdims=True)
            y = blk * lax.rsqrt(ms + EPS) * gain_ref[...]
            if rope:
                y = (y * cos_ref[...] + pltpu.roll(y, hd - 1, 1) * se_ref[...]
                     + pltpu.roll(y, 1, 1) * so_ref[...])
            if mult != 1.0:
                y = y * mult
            out_ref[:, sl] = y.astype(out_ref.dtype)

    def qk_path(gain_ref, out_ref, mult):
        @pl.when(i < n_ctx_tiles)
        def _():
            head_norm(gain_ref, out_ref, mult, False)

        @pl.when(i >= n_ctx_tiles)
        def _():
            head_norm(gain_ref, out_ref, mult, True)

    @pl.when(j < nq)
    def _():
        qk_path(qg_ref, q_ref, q_scale)

    @pl.when(jnp.logical_and(j >= nq, j < nq + nk))
    def _():
        qk_path(kg_ref, k_ref, 1.0)

    @pl.when(jnp.logical_and(j >= nq + nk, j < nq + nk + nv))
    def _():
        v_ref[...] = acc

    @pl.when(j >= nq + nk + nv)
    def _():
        u_ref[...] = acc.astype(u_ref.dtype)


def _rope_tables(n_tok, hd):
    axis_dim = hd // 2
    rows = n_tok // GRID_W
    row_idx = jnp.repeat(jnp.arange(rows, dtype=F32), GRID_W)
    col_idx = jnp.tile(jnp.arange(GRID_W, dtype=F32), rows)
    freqs = ROPE_THETA ** (-jnp.arange(0, axis_dim, 2, dtype=F32) / axis_dim)
    ang = jnp.concatenate([row_idx[:, None] * freqs, col_idx[:, None] * freqs], axis=-1)
    cos = jnp.repeat(jnp.cos(ang), 2, axis=-1)
    sin = jnp.repeat(jnp.sin(ang), 2, axis=-1)
    even = (jnp.arange(hd) % 2 == 0)[None, :]
    return cos, jnp.where(even, -sin, 0.0), jnp.where(even, 0.0, sin)


def _project(h, w_in, q_g, k_g, rope, layer, dims):
    t, d = h.shape
    hd, aw, kvw, fw = dims["hd"], dims["aw"], dims["kvw"], dims["fw"]
    tn = _tile(math.gcd(math.gcd(aw, kvw), fw), 512)
    rows = _Rows(dims["t_ctx"], dims["dec_seq"], _tile(min(dims["t_ctx"], dims["dec_seq"]), 1024))
    tm = rows.tm
    nq, nk, nv, nu = aw // tn, kvw // tn, kvw // tn, fw // tn
    cos, se, so = rope
    kern = functools.partial(_proj_kernel, nq=nq, nk=nk, nv=nv, n_ctx_tiles=rows.n_ctx_tiles,
                             hd=hd, q_scale=float(hd) ** -0.5)

    def clampj(lo, n):
        return lambda i, j: (i, jnp.clip(j - lo, 0, n - 1))

    rope_spec = pl.BlockSpec((tm, hd), lambda i, j: (rows.seq_tile(i), 0))
    return pl.pallas_call(
        kern,
        grid=(t // tm, nq + nk + nv + nu),
        in_specs=[
            pl.BlockSpec((tm, d), lambda i, j: (i, 0)),
            pl.BlockSpec((None, d, tn), lambda i, j: (layer, 0, j)),
            pl.BlockSpec((None, 1, hd), lambda i, j: (layer, 0, 0)),
            pl.BlockSpec((None, 1, hd), lambda i, j: (layer, 0, 0)),
            rope_spec, rope_spec, rope_spec,
        ],
        out_specs=[
            pl.BlockSpec((tm, tn), clampj(0, nq)),
            pl.BlockSpec((tm, tn), clampj(nq, nk)),
            pl.BlockSpec((tm, tn), clampj(nq + nk, nv)),
            pl.BlockSpec((tm, tn), clampj(nq + nk + nv, nu)),
        ],
        out_shape=[
            jax.ShapeDtypeStruct((t, aw), BF16),
            jax.ShapeDtypeStruct((t, kvw), F32),
            jax.ShapeDtypeStruct((t, kvw), F32),
            jax.ShapeDtypeStruct((t, fw), BF16),
        ],
        compiler_params=_cp("arbitrary", "arbitrary"),
        name="in_projection",
    )(h, w_in, q_g, k_g, cos, se, so)


def _softmax_pv(score_blocks, value_blocks):
    m = score_blocks[0].max(axis=-1, keepdims=True)
    for s in score_blocks[1:]:
        m = jnp.maximum(m, s.max(axis=-1, keepdims=True))
    den = None
    out = None
    for s, v in zip(score_blocks, value_blocks):
        p = jnp.exp(s - m)
        ps = p.sum(axis=-1, keepdims=True)
        pv = jnp.dot(p.astype(BF16), v, preferred_element_type=F32)
        den = ps if den is None else den + ps
        out = pv if out is None else out + pv
    return out * (1.0 / den)


def _attn_ctx_kernel(q_ref, k_ref, v_ref, mix_ref, o_ref, *, nkv, qpk, hd):
    s_len = q_ref.shape[0]
    for kv in range(nkv):
        kb = k_ref[:, kv * hd:(kv + 1) * hd].astype(BF16)
        vb = v_ref[:, kv * hd:(kv + 1) * hd].astype(BF16)
        q3 = jnp.concatenate(
            [q_ref[:, (kv * qpk + g) * hd:(kv * qpk + g + 1) * hd] for g in range(qpk)], axis=0)
        sc = lax.dot_general(q3, kb, _NT, preferred_element_type=F32)
        o = _softmax_pv([sc], [vb])
        for g in range(qpk):
            o_ref[:, (kv * qpk + g) * hd:(kv * qpk + g + 1) * hd] = (
                o[g * s_len:(g + 1) * s_len].astype(o_ref.dtype))


def _attn_lat_kernel(q_ref, k_ref, v_ref, ck_ref, cv_ref, mix_ref, o_ref, *, qpk, hd):
    tq = q_ref.shape[0]
    q3 = jnp.concatenate([q_ref[:, g * hd:(g + 1) * hd] for g in range(qpk)], axis=0)
    kb = k_ref[...].astype(BF16)
    ckb = ck_ref[...].astype(BF16)
    s1 = lax.dot_general(q3, kb, _NT, preferred_element_type=F32)
    s2 = lax.dot_general(q3, ckb, _NT, preferred_element_type=F32)
    o = _softmax_pv([s1, s2], [v_ref[...].astype(BF16), cv_ref[...].astype(BF16)])
    for g in range(qpk):
        o_ref[:, g * hd:(g + 1) * hd] = o[g * tq:(g + 1) * tq].astype(o_ref.dtype)


def _attention(q, k, v, cache_k4, cache_v4, mix, layer, dims):
    t, aw = q.shape
    hd, nkv, qpk = dims["hd"], dims["nkv"], dims["qpk"]
    seq, dec_seq, t_ctx, past = dims["seq"], dims["dec_seq"], dims["t_ctx"], dims["past"]
    n_ctx = t_ctx // seq
    kvw = nkv * hd
    mix = pl.pallas_call(
        functools.partial(_attn_ctx_kernel, nkv=nkv, qpk=qpk, hd=hd),
        grid=(n_ctx,),
        in_specs=[
            pl.BlockSpec((seq, aw), lambda b: (b, 0)),
            pl.BlockSpec((seq, kvw), lambda b: (b, 0)),
            pl.BlockSpec((seq, kvw), lambda b: (b, 0)),
            pl.BlockSpec(memory_space=pl.ANY),
        ],
        out_specs=pl.BlockSpec((seq, aw), lambda b: (b, 0)),
        out_shape=jax.ShapeDtypeStruct(mix.shape, mix.dtype),
        input_output_aliases={3: 0},
        compiler_params=_cp("arbitrary"),
        name="attention_context",
    )(q, k, v, mix)

    assert t_ctx % dec_seq == 0
    n_lat = (t - t_ctx) // dec_seq
    tq = _tile(dec_seq, 512)
    nqb = dec_seq // tq
    q_off = t_ctx // tq
    k_off = t_ctx // dec_seq
    qw = qpk * hd
    return pl.pallas_call(
        functools.partial(_attn_lat_kernel, qpk=qpk, hd=hd),
        grid=(n_lat, nkv, nqb),
        in_specs=[
            pl.BlockSpec((tq, qw), lambda b, kv, qi: (q_off + b * nqb + qi, kv)),
            pl.BlockSpec((dec_seq, hd), lambda b, kv, qi: (k_off + b, kv)),
            pl.BlockSpec((dec_seq, hd), lambda b, kv, qi: (k_off + b, kv)),
            pl.BlockSpec((None, None, past, hd), lambda b, kv, qi: (b, layer, 0, kv)),
            pl.BlockSpec((None, None, past, hd), lambda b, kv, qi: (b, layer, 0, kv)),
            pl.BlockSpec(memory_space=pl.ANY),
        ],
        out_specs=pl.BlockSpec((tq, qw), lambda b, kv, qi: (q_off + b * nqb + qi, kv)),
        out_shape=jax.ShapeDtypeStruct(mix.shape, mix.dtype),
        input_output_aliases={5: 0},
        compiler_params=_cp("arbitrary", "arbitrary", "arbitrary"),
        name="attention_latent",
    )(q, k, v, cache_k4, cache_v4, mix)


def _fourier_kernel(u_ref, dpos_ref, dch_ref, wf_ref, mix_ref, o_ref, tt_ref, *, ng, gd, norm):
    s_len = u_ref.shape[0]
    for g in range(ng):
        sl = slice(g * gd, (g + 1) * gd)
        t = jnp.dot(u_ref[:, sl], dch_ref[...], preferred_element_type=F32)
        tt_ref[0:s_len, sl] = t[:, :gd].astype(BF16)
        tt_ref[s_len:2 * s_len, sl] = t[:, gd:].astype(BF16)
    f = (jnp.dot(dpos_ref[...], tt_ref[...], preferred_element_type=F32) * norm).astype(BF16)
    for g in range(ng):
        sl = slice(g * gd, (g + 1) * gd)
        o_ref[:, sl] = jnp.dot(f[:, sl], wf_ref[g].astype(BF16),
                               preferred_element_type=F32).astype(o_ref.dtype)


def _dft_tables(n):
    jk = (jnp.arange(n, dtype=jnp.int32)[:, None] * jnp.arange(n, dtype=jnp.int32)[None, :]) % n
    ang = jk.astype(F32) * (2.0 * jnp.pi / n)
    return jnp.cos(ang), jnp.sin(ang)


def _fourier(u, w_fourier, mix, layer, s_len, row_off, n_seq, tables):
    ng, gd = w_fourier.shape[1], w_fourier.shape[2]
    fw = ng * gd
    assert (mix.shape[1] - fw) % fw == 0 and row_off % s_len == 0
    col = (mix.shape[1] - fw) // fw
    dpos, dch = tables
    off = row_off // s_len
    return pl.pallas_call(
        functools.partial(_fourier_kernel, ng=ng, gd=gd, norm=float(s_len * gd) ** -0.5),
        grid=(n_seq,),
        in_specs=[
            pl.BlockSpec((s_len, fw), lambda b: (off + b, 0)),
            pl.BlockSpec((s_len, 2 * s_len), lambda b: (0, 0)),
            pl.BlockSpec((gd, 2 * gd), lambda b: (0, 0)),
            pl.BlockSpec((None, ng, gd, gd), lambda b: (layer, 0, 0, 0)),
            pl.BlockSpec(memory_space=pl.ANY),
        ],
        out_specs=pl.BlockSpec((s_len, fw), lambda b: (off + b, col)),
        out_shape=jax.ShapeDtypeStruct(mix.shape, mix.dtype),
        input_output_aliases={4: 0},
        scratch_shapes=[pltpu.VMEM((2 * s_len, fw), BF16)],
        compiler_params=_cp("arbitrary"),
        name="fourier_mix",
    )(u, dpos, dch, w_fourier, mix)


def _wout_kernel(m_ref, w_ref, x_ref, g_ref, o_ref):
    acc = jnp.dot(m_ref[...], w_ref[...].astype(BF16), preferred_element_type=F32)
    o_ref[...] = x_ref[...] + g_ref[...] * acc


def _out_project(mix, w_out, x, mods, layer, dims):
    t, d = x.shape
    mw = mix.shape[1]
    rows = _Rows(dims["t_ctx"], dims["dec_seq"], _tile(min(dims["t_ctx"], dims["dec_seq"]), 1024))
    tm = rows.tm
    tn = _tile(d, 512)
    return pl.pallas_call(
        _wout_kernel,
        grid=(t // tm, d // tn),
        in_specs=[
            pl.BlockSpec((tm, mw), lambda i, j: (i, 0)),
            pl.BlockSpec((None, mw, tn), lambda i, j: (layer, 0, j)),
            pl.BlockSpec((tm, tn), lambda i, j: (i, j)),
            _mod_spec(rows, layer, 2, tn, col=lambda j: j),
        ],
        out_specs=pl.BlockSpec((tm, tn), lambda i, j: (i, j)),
        out_shape=jax.ShapeDtypeStruct((t, d), F32),
        compiler_params=_cp("arbitrary", "arbitrary"),
        name="out_projection",
    )(mix, w_out, x, mods)


def _top2_sum(a, b, c, d):
    hi1, lo1 = jnp.maximum(a, b), jnp.minimum(a, b)
    hi2, lo2 = jnp.maximum(c, d), jnp.minimum(c, d)
    return jnp.maximum(hi1, hi2) + jnp.maximum(jnp.minimum(hi1, hi2), jnp.maximum(lo1, lo2))


def _norm_router_kernel(x_ref, g_ref, sc_ref, sh_ref, wr_ref, rb_ref, hp_ref, idx_ref, wt_ref,
                        *, n_exp, n_grp):
    hf = _norm_mod(x_ref[...], g_ref[...], sc_ref[...], sh_ref[...])
    half = hf.shape[1] // 2
    hp_ref[...] = _pack_pair(hf[:, :half], hf[:, half:])

    hi = hf.astype(BF16)
    lo = (hf - hi.astype(F32)).astype(BF16)
    w = wr_ref[...]
    whi = w.astype(BF16)
    wlo = (w - whi.astype(F32)).astype(BF16)
    lg = (lax.dot_general(whi, hi, _NT, preferred_element_type=F32)
          + lax.dot_general(whi, lo, _NT, preferred_element_type=F32)
          + lax.dot_general(wlo, hi, _NT, preferred_element_type=F32))
    ex = jnp.exp(lg - lg.max(axis=0, keepdims=True))
    aff = ex * (1.0 / ex.sum(axis=0, keepdims=True))
    sel = aff + rb_ref[...]

    per = n_exp // n_grp
    row = lambda a, e: a[e:e + 1, :]
    assert per == 4, "group score uses a 4-element top-2 network"
    best_s = _top2_sum(*[row(sel, e) for e in range(per)])
    best_g = jnp.zeros_like(best_s, dtype=jnp.int32)
    for g in range(1, n_grp):
        s = _top2_sum(*[row(sel, g * per + e) for e in range(per)])
        upd = s > best_s
        best_g = jnp.where(upd, g, best_g)
        best_s = jnp.where(upd, s, best_s)

    neg = jnp.float32(-jnp.inf)
    picks = []
    for _ in range(2):
        bv = jnp.full_like(best_s, neg)
        bi = jnp.zeros_like(best_g)
        ba = jnp.zeros_like(best_s)
        for e in range(n_exp):
            ok = best_g == (e // per)
            for (pi, _) in picks:
                ok = jnp.logical_and(ok, pi != e)
            val = jnp.where(ok, row(sel, e), neg)
            upd = val > bv
            bv = jnp.where(upd, val, bv)
            bi = jnp.where(upd, e, bi)
            ba = jnp.where(upd, row(aff, e), ba)
        picks.append((bi, ba))
    (i0, a0), (i1, a1) = picks
    inv = 1.0 / (a0 + a1)
    idx_ref[0:1, :] = i0
    idx_ref[1:2, :] = i1
    wt_ref[0:1, :] = a0 * inv
    wt_ref[1:2, :] = a1 * inv


def _norm_router(x, gain, mods, w_router_t, router_bias, layer, dims):
    t, d = x.shape
    n_exp = w_router_t.shape[0]
    rows = _Rows(dims["t_ctx"], dims["dec_seq"], _tile(min(dims["t_ctx"], dims["dec_seq"]), 256))
    tm = rows.tm
    return pl.pallas_call(
        functools.partial(_norm_router_kernel, n_exp=n_exp, n_grp=N_EXPERT_GROUPS),
        grid=(t // tm,),
        in_specs=[
            pl.BlockSpec((tm, d), lambda i: (i, 0)),
            pl.BlockSpec((None, 1, d), lambda i: (layer, 0, 0)),
            _mod_spec(rows, layer, 4, d),
            _mod_spec(rows, layer, 3, d),
            pl.BlockSpec((n_exp, d), lambda i: (0, 0)),
            pl.BlockSpec((n_exp, 1), lambda i: (0, 0)),
        ],
        out_specs=[
            pl.BlockSpec((tm, d // 2), lambda i: (i, 0)),
            pl.BlockSpec((2, tm), lambda i: (0, i)),
            pl.BlockSpec((2, tm), lambda i: (0, i)),
        ],
        out_shape=[
            jax.ShapeDtypeStruct((t, d // 2), U32),
            jax.ShapeDtypeStruct((2, t), jnp.int32),
            jax.ShapeDtypeStruct((2, t), F32),
        ],
        compiler_params=_cp("arbitrary"),
        name="norm_router",
    )(x, gain, mods, mods, w_router_t, router_bias.reshape(n_exp, 1))


def _dispatch_plan(idx, n_exp, item_rows, sub_rows, n_items):
    t = idx.shape[1]
    e_a = idx.reshape(-1)
    onehot = (e_a[:, None] == jnp.arange(n_exp, dtype=jnp.int32)[None, :]).astype(jnp.int32)
    csum = jnp.cumsum(onehot, axis=0)
    counts = csum[-1]
    rank = jnp.take_along_axis(csum, e_a[:, None], axis=1)[:, 0] - 1
    items_e = (counts + item_rows - 1) // item_rows
    item_end = jnp.cumsum(items_e)
    item_start = item_end - items_e
    pos = item_start[e_a] * item_rows + rank
    tok = jnp.tile(jnp.arange(t, dtype=jnp.int32), 2)
    slot_tok = jnp.zeros((n_items * item_rows,), jnp.int32).at[pos].set(tok)
    w = jnp.arange(n_items, dtype=jnp.int32)
    item_exp = jnp.minimum(jnp.searchsorted(item_end, w, side="right"), n_exp - 1).astype(jnp.int32)
    rows = jnp.clip(counts[item_exp] - (w - item_start[item_exp]) * item_rows, 0, item_rows)
    n_sub = jnp.where(w < item_end[-1], (rows + sub_rows - 1) // sub_rows, 0).astype(jnp.int32)
    n_valid = item_end[-1].astype(jnp.int32).reshape(1)
    return slot_tok, pos.reshape(2, t), item_exp, n_sub, n_valid


def _row_copy(src_hbm, dst, sem, src_row, dst_row, n=1):
    return pltpu.make_async_copy(src_hbm.at[pl.ds(src_row, n)], dst.at[pl.ds(dst_row, n)], sem)


def _zero_tail(o_ref, first_sub, n_sub_total, sub):
    def body(s, carry):
        o_ref[pl.ds(pl.multiple_of(s * sub, sub), sub), :] = jnp.zeros((sub, o_ref.shape[1]), o_ref.dtype)
        return carry

    lax.fori_loop(first_sub, n_sub_total, body, 0)


def _ffn_up_kernel(te_ref, ns_ref, nv_ref, tok_ref, tok_next_ref, hp_ref, w1_ref, w3_ref, o_ref,
                   xbuf, xl_ref, xr_ref, wb1_ref, wb3_ref, sem, *, sub):
    w = pl.program_id(0)
    c = pl.program_id(1)
    nv = nv_ref[0]
    n_sub_total = o_ref.shape[0] // sub
    half = xl_ref.shape[1]

    def start_gather(toks, n_sub):
        def issue(r, carry):
            _row_copy(hp_ref, xbuf, sem, toks[0, 0, r], r).start()
            return carry

        lax.fori_loop(0, n_sub * sub, issue, 0)

    @pl.when(jnp.logical_and(w == 0, c == 0))
    def _():
        start_gather(tok_ref, ns_ref[0])

    @pl.when(jnp.logical_and(c == 0, w < nv))
    def _():
        def land(s, carry):
            _row_copy(hp_ref, xbuf, sem, 0, 0, sub).wait()
            return carry

        lax.fori_loop(0, ns_ref[w], land, 0)

        def unpack(s, carry):
            rows = pl.ds(pl.multiple_of(s * sub, sub), sub)
            lo, hi = _unpack_pair(xbuf[rows, :])
            xl_ref[rows, :] = lo.astype(BF16)
            xr_ref[rows, :] = hi.astype(BF16)
            return carry

        lax.fori_loop(0, ns_ref[w], unpack, 0)

        @pl.when(w + 1 < nv)
        def _():
            start_gather(tok_next_ref, ns_ref[jnp.minimum(w + 1, ns_ref.shape[0] - 1)])

    @pl.when(w < nv)
    def _():
        wb1_ref[...] = w1_ref[...].astype(BF16)
        wb3_ref[...] = w3_ref[...].astype(BF16)

        def body(s, carry):
            rows = pl.ds(pl.multiple_of(s * sub, sub), sub)
            xl = xl_ref[rows, :]
            xr = xr_ref[rows, :]
            a = (jnp.dot(xl, wb1_ref[:half, :], preferred_element_type=F32)
                 + jnp.dot(xr, wb1_ref[half:, :], preferred_element_type=F32))
            b = (jnp.dot(xl, wb3_ref[:half, :], preferred_element_type=F32)
                 + jnp.dot(xr, wb3_ref[half:, :], preferred_element_type=F32))
            o_ref[rows, :] = (a * jax.nn.sigmoid(a) * b).astype(o_ref.dtype)
            return carry

        lax.fori_loop(0, ns_ref[w], body, 0)
        _zero_tail(o_ref, ns_ref[w], n_sub_total, sub)

    @pl.when(w >= nv)
    def _():
        o_ref[...] = jnp.zeros_like(o_ref)


def _ffn_down_kernel(te_ref, ns_ref, nv_ref, h_ref, wl_ref, wr_ref, o_ref, wbl_ref, wbr_ref, *, sub):
    w = pl.program_id(0)
    n_sub_total = o_ref.shape[0] // sub

    @pl.when(w < nv_ref[0])
    def _():
        wbl_ref[...] = wl_ref[...].astype(BF16)
        wbr_ref[...] = wr_ref[...].astype(BF16)

        def body(s, carry):
            rows = pl.ds(pl.multiple_of(s * sub, sub), sub)
            h = h_ref[rows, :]
            yl = jnp.dot(h, wbl_ref[...], preferred_element_type=F32)
            yr = jnp.dot(h, wbr_ref[...], preferred_element_type=F32)
            o_ref[rows, :] = _pack_pair(yl, yr)
            return carry

        lax.fori_loop(0, ns_ref[w], body, 0)
        _zero_tail(o_ref, ns_ref[w], n_sub_total, sub)

    @pl.when(w >= nv_ref[0])
    def _():
        o_ref[...] = jnp.zeros_like(o_ref)


def _expert_ffn(hp, slot_tok, item_exp, n_sub, n_valid, w1, w3, w2, layer, item_rows, sub_rows):
    t, dh = hp.shape
    d = 2 * dh
    f = w1.shape[3]
    n_items = item_exp.shape[0]
    r_total = n_items * item_rows
    tf = _tile(f, 256)
    nf = f // tf
    toks = slot_tok.reshape(n_items, 1, item_rows)

    def live(w, nv):
        return jnp.minimum(w, nv[0] - 1)

    def chunk(w, c, nv, n_chunks):
        return jnp.where(w < nv[0], c, n_chunks - 1)

    hid = pl.pallas_call(
        functools.partial(_ffn_up_kernel, sub=sub_rows),
        grid_spec=pltpu.PrefetchScalarGridSpec(
            num_scalar_prefetch=3,
            grid=(n_items, nf),
            in_specs=[
                pl.BlockSpec((1, 1, item_rows), lambda w, c, te, ns, nv: (w, 0, 0),
                             memory_space=pltpu.SMEM),
                pl.BlockSpec((1, 1, item_rows),
                             lambda w, c, te, ns, nv: (jnp.minimum(w + 1, n_items - 1), 0, 0),
                             memory_space=pltpu.SMEM),
                pl.BlockSpec(memory_space=pl.ANY),
                pl.BlockSpec((None, None, d, tf),
                             lambda w, c, te, ns, nv: (layer, te[live(w, nv)], 0, chunk(w, c, nv, nf))),
                pl.BlockSpec((None, None, d, tf),
                             lambda w, c, te, ns, nv: (layer, te[live(w, nv)], 0, chunk(w, c, nv, nf))),
            ],
            out_specs=pl.BlockSpec((item_rows, tf), lambda w, c, te, ns, nv: (w, c)),
            scratch_shapes=[
                pltpu.VMEM((item_rows, dh), U32),
                pltpu.VMEM((item_rows, dh), BF16),
                pltpu.VMEM((item_rows, dh), BF16),
                pltpu.VMEM((d, tf), BF16),
                pltpu.VMEM((d, tf), BF16),
                pltpu.SemaphoreType.DMA,
            ],
        ),
        out_shape=jax.ShapeDtypeStruct((r_total, f), BF16),
        compiler_params=_cp("arbitrary", "arbitrary"),
        name="expert_ffn_up",
    )(item_exp, n_sub, n_valid, toks, toks, hp, w1, w3)

    tn = _tile(dh, 512)
    nn = dh // tn
    return pl.pallas_call(
        functools.partial(_ffn_down_kernel, sub=sub_rows),
        grid_spec=pltpu.PrefetchScalarGridSpec(
            num_scalar_prefetch=3,
            grid=(n_items, nn),
            in_specs=[
                pl.BlockSpec((item_rows, f), lambda w, c, te, ns, nv: (live(w, nv), 0)),
                pl.BlockSpec((None, None, f, tn),
                             lambda w, c, te, ns, nv: (layer, te[live(w, nv)], 0, chunk(w, c, nv, nn))),
                pl.BlockSpec((None, None, f, tn),
                             lambda w, c, te, ns, nv: (layer, te[live(w, nv)], 0,
                                                       nn + chunk(w, c, nv, nn))),
            ],
            out_specs=pl.BlockSpec((item_rows, tn), lambda w, c, te, ns, nv: (w, c)),
            scratch_shapes=[pltpu.VMEM((f, tn), BF16), pltpu.VMEM((f, tn), BF16)],
        ),
        out_shape=jax.ShapeDtypeStruct((r_total, dh), U32),
        compiler_params=_cp("arbitrary", "arbitrary"),
        name="expert_ffn_down",
    )(item_exp, n_sub, n_valid, hid, w2, w2)


def _combine_kernel(pos_ref, pos_next_ref, y_ref, x_ref, g_ref, wt_ref, o_ref, ybuf, sem, *, tm):
    i = pl.program_id(0)
    slot = i % 2

    def start_gather(poss, dst_slot):
        def issue(r, carry):
            _row_copy(y_ref, ybuf.at[dst_slot], sem.at[dst_slot], poss[0, 0, r], r).start()
            return carry

        lax.fori_loop(0, 2 * tm, issue, 0)

    @pl.when(i == 0)
    def _():
        start_gather(pos_ref, 0)

    @pl.when(i + 1 < pl.num_programs(0))
    def _():
        start_gather(pos_next_ref, 1 - slot)

    _row_copy(y_ref, ybuf.at[slot], sem.at[slot], 0, 0, 2 * tm).wait()

    half = ybuf.shape[2]
    w0 = wt_ref[:, 0:1]
    w1 = wt_ref[:, 1:2]
    l0, r0 = _unpack_pair(ybuf[slot, 0:tm, :])
    l1, r1 = _unpack_pair(ybuf[slot, tm:2 * tm, :])
    o_ref[:, :half] = x_ref[:, :half] + g_ref[:, :half] * (w0 * l0 + w1 * l1)
    o_ref[:, half:] = x_ref[:, half:] + g_ref[:, half:] * (w0 * r0 + w1 * r1)


def _combine(y, pos, wts, x, mods, layer, dims):
    t, d = x.shape
    rows = _Rows(dims["t_ctx"], dims["dec_seq"], _tile(min(dims["t_ctx"], dims["dec_seq"]), 256))
    tm = rows.tm
    nt = t // tm
    pos_tiles = pos.reshape(2, nt, tm).transpose(1, 0, 2).reshape(nt, 1, 2 * tm)
    return pl.pallas_call(
        functools.partial(_combine_kernel, tm=tm),
        grid=(nt,),
        in_specs=[
            pl.BlockSpec((1, 1, 2 * tm), lambda i: (i, 0, 0), memory_space=pltpu.SMEM),
            pl.BlockSpec((1, 1, 2 * tm), lambda i: (jnp.minimum(i + 1, nt - 1), 0, 0),
                         memory_space=pltpu.SMEM),
            pl.BlockSpec(memory_space=pl.ANY),
            pl.BlockSpec((tm, d), lambda i: (i, 0)),
            _mod_spec(rows, layer, 5, d),
            pl.BlockSpec((tm, 2), lambda i: (i, 0)),
        ],
        out_specs=pl.BlockSpec((tm, d), lambda i: (i, 0)),
        out_shape=jax.ShapeDtypeStruct((t, d), F32),
        scratch_shapes=[pltpu.VMEM((2, 2 * tm, d // 2), U32), pltpu.SemaphoreType.DMA((2,))],
        compiler_params=_cp("arbitrary"),
        name="moe_combine",
    )(pos_tiles, pos_tiles, y, x, mods, wts.T)


MOE_ITEM_ROWS = 1024
MOE_SUB_ROWS = 256


def kernel(x_prompt, x_sample, cache_k, cache_v, c, c_ctx, w_ada, b_ada, norm_mix, norm_ffn, w_in, q_norm, k_norm, w_fourier, w_out, w_router, router_bias, w1, w3, w2):
    batch, seq, d = x_prompt.shape
    dec_batch, dec_seq, _ = x_sample.shape
    depth = w_in.shape[0]
    past, nkv, hd = cache_k.shape[2], cache_k.shape[3], cache_k.shape[4]
    ng, gd = w_fourier.shape[1], w_fourier.shape[2]
    fw = ng * gd
    kvw = nkv * hd
    aw = w_out.shape[1] - fw
    n_exp = w_router.shape[1]
    t_ctx = batch * seq
    t_lat = dec_batch * dec_seq
    t = t_ctx + t_lat
    assert hd == LANE and dec_batch + 1 <= COND_ROWS and w_in.shape[2] == aw + 2 * kvw + fw
    dims = dict(hd=hd, aw=aw, kvw=kvw, fw=fw, nkv=nkv, qpk=aw // kvw, seq=seq, dec_seq=dec_seq,
                t_ctx=t_ctx, past=past)

    cond = jnp.zeros((COND_ROWS, d), F32).at[0].set(c_ctx).at[1:1 + dec_batch].set(c)
    mods = _modulation(cond, w_ada, b_ada).reshape(depth, COND_ROWS, N_MOD, 1, d)

    rope = _rope_tables(dec_seq, hd)
    cs_c, sn_c = _dft_tables(seq)
    cs_l, sn_l = _dft_tables(dec_seq)
    cs_g, sn_g = _dft_tables(gd)
    dch = jnp.concatenate([cs_g, sn_g], axis=1).astype(BF16)
    dft_ctx = (jnp.concatenate([cs_c, -sn_c], axis=1).astype(BF16), dch)
    dft_lat = (jnp.concatenate([cs_l, -sn_l], axis=1).astype(BF16), dch)

    cache_k4 = cache_k.reshape(dec_batch, depth, past, kvw)
    cache_v4 = cache_v.reshape(dec_batch, depth, past, kvw)
    w_router_t = w_router.T
    n_moe_items = (2 * t) // MOE_ITEM_ROWS + n_exp
    mix = jnp.zeros((t, aw + fw), BF16)
    norm_mix3 = norm_mix.reshape(depth, 1, d)
    norm_ffn3 = norm_ffn.reshape(depth, 1, d)
    q_norm3 = q_norm.reshape(depth, 1, hd)
    k_norm3 = k_norm.reshape(depth, 1, hd)

    x = jnp.concatenate([x_prompt.reshape(t_ctx, d), x_sample.reshape(t_lat, d)], axis=0)
    new_k, new_v = [], []
    for l in range(depth):
        h = _norm_modulate(x, norm_mix3, mods, l, t_ctx, dec_seq)
        q, k, v, u = _project(h, w_in, q_norm3, k_norm3, rope, l, dims)
        new_k.append(k[:t_ctx].reshape(batch, seq, nkv, hd))
        new_v.append(v[:t_ctx].reshape(batch, seq, nkv, hd))
        mix = _attention(q, k, v, cache_k4, cache_v4, mix, l, dims)
        mix = _fourier(u, w_fourier, mix, l, seq, 0, batch, dft_ctx)
        mix = _fourier(u, w_fourier, mix, l, dec_seq, t_ctx, dec_batch, dft_lat)
        x = _out_project(mix, w_out, x, mods, l, dims)

        hp, idx, wts = _norm_router(x, norm_ffn3, mods, w_router_t, router_bias, l, dims)
        slot_tok, pos, item_exp, n_sub, n_valid = _dispatch_plan(
            idx, n_exp, MOE_ITEM_ROWS, MOE_SUB_ROWS, n_moe_items)
        y = _expert_ffn(hp, slot_tok, item_exp, n_sub, n_valid, w1, w3, w2, l,
                        MOE_ITEM_ROWS, MOE_SUB_ROWS)
        x = _combine(y, pos, wts, x, mods, l, dims)

    y_prompt = x[:t_ctx].reshape(batch, seq, d)
    y_sample = x[t_ctx:].reshape(dec_batch, dec_seq, d)
    return (y_prompt, y_sample, jnp.stack(new_k, axis=1), jnp.stack(new_v, axis=1))
```

```python
import functools
import math

import jax
import jax.numpy as jnp
from jax import lax
from jax.experimental import pallas as pl
from jax.experimental.pallas import tpu as pltpu

GRID_W = 64
N_EXPERT_GROUPS = 4
ROPE_THETA = 10000.0
EPS = 1e-6
N_MOD = 6
COND_ROWS = 8

VMEM_LIMIT_BYTES = 56 * 1024 * 1024
LANE = 128

BF16 = jnp.bfloat16
F32 = jnp.float32
U32 = jnp.uint32

_NT = (((1,), (1,)), ((), ()))


def _cp(*sem):
    return pltpu.CompilerParams(dimension_semantics=sem, vmem_limit_bytes=VMEM_LIMIT_BYTES)


def _tile(n, pref):
    t = min(n, pref)
    while n % t:
        t -= 1
    return t


def _pack_pair(a, b):
    ua = lax.bitcast_convert_type(a.astype(BF16).astype(F32), U32)
    ub = lax.bitcast_convert_type(b.astype(BF16).astype(F32), U32)
    return (ua >> 16) | (ub & jnp.uint32(0xFFFF0000))


def _unpack_pair(w):
    lo = lax.bitcast_convert_type(w << 16, F32)
    hi = lax.bitcast_convert_type(w & jnp.uint32(0xFFFF0000), F32)
    return lo, hi


def _mod_kernel(c_ref, w_ref, b_ref, o_ref):
    c = c_ref[...]
    s = (c * jax.nn.sigmoid(c)).astype(BF16)
    o_ref[...] = jnp.dot(s, w_ref[...].astype(BF16), preferred_element_type=F32) + b_ref[...]


def _modulation(cond, w_ada, b_ada):
    depth, d, n = w_ada.shape
    tn = _tile(n, 512)
    return pl.pallas_call(
        _mod_kernel,
        grid=(depth, n // tn),
        in_specs=[
            pl.BlockSpec((COND_ROWS, d), lambda l, j: (0, 0)),
            pl.BlockSpec((None, d, tn), lambda l, j: (l, 0, j)),
            pl.BlockSpec((None, 1, tn), lambda l, j: (l, 0, j)),
        ],
        out_specs=pl.BlockSpec((None, COND_ROWS, tn), lambda l, j: (l, 0, j)),
        out_shape=jax.ShapeDtypeStruct((depth, COND_ROWS, n), F32),
        compiler_params=_cp("arbitrary", "arbitrary"),
        name="modulation",
    )(cond, w_ada, b_ada.reshape(depth, 1, n))


class _Rows:
    def __init__(self, t_ctx, dec_seq, tm):
        assert t_ctx % tm == 0 and dec_seq % tm == 0
        self.tm = tm
        self.n_ctx_tiles = t_ctx // tm
        self.tiles_per_seq = dec_seq // tm

    def cond(self, i):
        lat = 1 + (i - self.n_ctx_tiles) // self.tiles_per_seq
        return jnp.where(i < self.n_ctx_tiles, 0, lat)

    def seq_tile(self, i):
        return jnp.maximum(i - self.n_ctx_tiles, 0) % self.tiles_per_seq


def _mod_spec(rows, layer, which, width, col=None):
    if col is None:
        return pl.BlockSpec((None, None, None, 1, width),
                            lambda i, *_: (layer, rows.cond(i), which, 0, 0))
    return pl.BlockSpec((None, None, None, 1, width),
                        lambda i, j, *_: (layer, rows.cond(i), which, 0, col(j)))


def _norm_mod(x, g, sc, sh):
    ms = jnp.mean(x * x, axis=-1, keepdims=True)
    y = x * lax.rsqrt(ms + EPS) * g
    return y * (1.0 + sc) + sh


def _norm_mod_kernel(x_ref, g_ref, sc_ref, sh_ref, o_ref):
    o_ref[...] = _norm_mod(x_ref[...], g_ref[...], sc_ref[...], sh_ref[...]).astype(o_ref.dtype)


def _norm_modulate(x, gain, mods, layer, t_ctx, dec_seq):
    t, d = x.shape
    rows = _Rows(t_ctx, dec_seq, _tile(min(t_ctx, dec_seq), 256))
    tm = rows.tm
    return pl.pallas_call(
        _norm_mod_kernel,
        grid=(t // tm,),
        in_specs=[
            pl.BlockSpec((tm, d), lambda i: (i, 0)),
            pl.BlockSpec((None, 1, d), lambda i: (layer, 0, 0)),
            _mod_spec(rows, layer, 1, d),
            _mod_spec(rows, layer, 0, d),
        ],
        out_specs=pl.BlockSpec((tm, d), lambda i: (i, 0)),
        out_shape=jax.ShapeDtypeStruct((t, d), BF16),
        compiler_params=_cp("arbitrary"),
        name="norm_modulate",
    )(x, gain, mods, mods)


def _proj_kernel(h_ref, w_ref, qg_ref, kg_ref, cos_ref, se_ref, so_ref,
                 q_ref, k_ref, v_ref, u_ref, *, nq, nk, nv, n_ctx_tiles, hd, q_scale):
    i = pl.program_id(0)
    j = pl.program_id(1)
    acc = jnp.dot(h_ref[...], w_ref[...].astype(BF16), preferred_element_type=F32)
    heads = acc.shape[1] // hd

    def head_norm(gain_ref, out_ref, mult, rope):
        for hh in range(heads):
            sl = slice(hh * hd, (hh + 1) * hd)
            blk = acc[:, sl]
            ms = jnp.mean(blk * blk, axis=-1, keepdims=True)
            y = blk * lax.rsqrt(ms + EPS) * gain_ref[...]
            if rope:
                y = (y * cos_ref[...] + pltpu.roll(y, hd - 1, 1) * se_ref[...]
                     + pltpu.roll(y, 1, 1) * so_ref[...])
            if mult != 1.0:
                y = y * mult
            out_ref[:, sl] = y.astype(out_ref.dtype)

    def qk_path(gain_ref, out_ref, mult):
        @pl.when(i < n_ctx_tiles)
        def _():
            head_norm(gain_ref, out_ref, mult, False)

        @pl.when(i >= n_ctx_tiles)
        def _():
            head_norm(gain_ref, out_ref, mult, True)

    @pl.when(j < nq)
    def _():
        qk_path(qg_ref, q_ref, q_scale)

    @pl.when(jnp.logical_and(j >= nq, j < nq + nk))
    def _():
        qk_path(kg_ref, k_ref, 1.0)

    @pl.when(jnp.logical_and(j >= nq + nk, j < nq + nk + nv))
    def _():
        v_ref[...] = acc

    @pl.when(j >= nq + nk + nv)
    def _():
        u_ref[...] = acc.astype(u_ref.dtype)


def _rope_tables(n_tok, hd):
    axis_dim = hd // 2
    rows = n_tok // GRID_W
    row_idx = jnp.repeat(jnp.arange(rows, dtype=F32), GRID_W)
    col_idx = jnp.tile(jnp.arange(GRID_W, dtype=F32), rows)
    freqs = ROPE_THETA ** (-jnp.arange(0, axis_dim, 2, dtype=F32) / axis_dim)
    ang = jnp.concatenate([row_idx[:, None] * freqs, col_idx[:, None] * freqs], axis=-1)
    cos = jnp.repeat(jnp.cos(ang), 2, axis=-1)
    sin = jnp.repeat(jnp.sin(ang), 2, axis=-1)
    even = (jnp.arange(hd) % 2 == 0)[None, :]
    return cos, jnp.where(even, -sin, 0.0), jnp.where(even, 0.0, sin)


def _project(h, w_in, q_g, k_g, rope, layer, dims):
    t, d = h.shape
    hd, aw, kvw, fw = dims["hd"], dims["aw"], dims["kvw"], dims["fw"]
    tn = _tile(math.gcd(math.gcd(aw, kvw), fw), 512)
    rows = _Rows(dims["t_ctx"], dims["dec_seq"], _tile(min(dims["t_ctx"], dims["dec_seq"]), 1024))
    tm = rows.tm
    nq, nk, nv, nu = aw // tn, kvw // tn, kvw // tn, fw // tn
    cos, se, so = rope
    kern = functools.partial(_proj_kernel, nq=nq, nk=nk, nv=nv, n_ctx_tiles=rows.n_ctx_tiles,
                             hd=hd, q_scale=float(hd) ** -0.5)

    def clampj(lo, n):
        return lambda i, j: (i, jnp.clip(j - lo, 0, n - 1))

    rope_spec = pl.BlockSpec((tm, hd), lambda i, j: (rows.seq_tile(i), 0))
    return pl.pallas_call(
        kern,
        grid=(t // tm, nq + nk + nv + nu),
        in_specs=[
            pl.BlockSpec((tm, d), lambda i, j: (i, 0)),
            pl.BlockSpec((None, d, tn), lambda i, j: (layer, 0, j)),
            pl.BlockSpec((None, 1, hd), lambda i, j: (layer, 0, 0)),
            pl.BlockSpec((None, 1, hd), lambda i, j: (layer, 0, 0)),
            rope_spec, rope_spec, rope_spec,
        ],
        out_specs=[
            pl.BlockSpec((tm, tn), clampj(0, nq)),
            pl.BlockSpec((tm, tn), clampj(nq, nk)),
            pl.BlockSpec((tm, tn), clampj(nq + nk, nv)),
            pl.BlockSpec((tm, tn), clampj(nq + nk + nv, nu)),
        ],
        out_shape=[
            jax.ShapeDtypeStruct((t, aw), BF16),
            jax.ShapeDtypeStruct((t, kvw), F32),
            jax.ShapeDtypeStruct((t, kvw), F32),
            jax.ShapeDtypeStruct((t, fw), BF16),
        ],
        compiler_params=_cp("arbitrary", "arbitrary"),
        name="in_projection",
    )(h, w_in, q_g, k_g, cos, se, so)


def _cache_store_kernel(k_ref, v_ref, kc_in_ref, vc_in_ref, kc_ref, vc_ref):
    kc_ref[...] = k_ref[...].reshape(kc_ref.shape)
    vc_ref[...] = v_ref[...].reshape(vc_ref.shape)


def _store_context_cache(k, v, kc, vc, layer, dims):
    seq, t_ctx = dims["seq"], dims["t_ctx"]
    kvw = k.shape[1]
    tm = _tile(t_ctx // seq, 4) * seq
    nb = tm // seq
    out_spec = pl.BlockSpec((nb, None, seq, kvw), lambda i: (i, layer, 0, 0))
    return pl.pallas_call(
        _cache_store_kernel,
        grid=(t_ctx // tm,),
        in_specs=[
            pl.BlockSpec((tm, kvw), lambda i: (i, 0)),
            pl.BlockSpec((tm, kvw), lambda i: (i, 0)),
            pl.BlockSpec(memory_space=pl.ANY),
            pl.BlockSpec(memory_space=pl.ANY),
        ],
        out_specs=[out_spec, out_spec],
        out_shape=[jax.ShapeDtypeStruct(kc.shape, kc.dtype), jax.ShapeDtypeStruct(vc.shape, vc.dtype)],
        input_output_aliases={2: 0, 3: 1},
        compiler_params=_cp("arbitrary"),
        name="context_cache_store",
    )(k, v, kc, vc)


def _softmax_pv(score_blocks, value_blocks):
    m = score_blocks[0].max(axis=-1, keepdims=True)
    for s in score_blocks[1:]:
        m = jnp.maximum(m, s.max(axis=-1, keepdims=True))
    den = None
    out = None
    for s, v in zip(score_blocks, value_blocks):
        p = jnp.exp(s - m)
        ps = p.sum(axis=-1, keepdims=True)
        pv = jnp.dot(p.astype(BF16), v, preferred_element_type=F32)
        den = ps if den is None else den + ps
        out = pv if out is None else out + pv
    return out * (1.0 / den)


def _attn_ctx_kernel(q_ref, k_ref, v_ref, mix_ref, o_ref, *, nkv, qpk, hd):
    s_len = q_ref.shape[0]
    for kv in range(nkv):
        kb = k_ref[:, kv * hd:(kv + 1) * hd].astype(BF16)
        vb = v_ref[:, kv * hd:(kv + 1) * hd].astype(BF16)
        q3 = jnp.concatenate(
            [q_ref[:, (kv * qpk + g) * hd:(kv * qpk + g + 1) * hd] for g in range(qpk)], axis=0)
        sc = lax.dot_general(q3, kb, _NT, preferred_element_type=F32)
        o = _softmax_pv([sc], [vb])
        for g in range(qpk):
            o_ref[:, (kv * qpk + g) * hd:(kv * qpk + g + 1) * hd] = (
                o[g * s_len:(g + 1) * s_len].astype(o_ref.dtype))


def _attn_lat_kernel(q_ref, k_ref, v_ref, ck_ref, cv_ref, mix_ref, o_ref, *, qpk, hd):
    kb = k_ref[...].astype(BF16)
    ckb = ck_ref[...].astype(BF16)
    vb = v_ref[...].astype(BF16)
    cvb = cv_ref[...].astype(BF16)
    for g in range(qpk):
        qg = q_ref[:, g * hd:(g + 1) * hd]
        s1 = lax.dot_general(qg, kb, _NT, preferred_element_type=F32)
        s2 = lax.dot_general(qg, ckb, _NT, preferred_element_type=F32)
        o_ref[:, g * hd:(g + 1) * hd] = _softmax_pv([s1, s2], [vb, cvb]).astype(o_ref.dtype)


def _attention(q, k, v, cache_k4, cache_v4, mix, layer, dims):
    t, aw = q.shape
    hd, nkv, qpk = dims["hd"], dims["nkv"], dims["qpk"]
    seq, dec_seq, t_ctx, past = dims["seq"], dims["dec_seq"], dims["t_ctx"], dims["past"]
    n_ctx = t_ctx // seq
    kvw = nkv * hd
    mix = pl.pallas_call(
        functools.partial(_attn_ctx_kernel, nkv=nkv, qpk=qpk, hd=hd),
        grid=(n_ctx,),
        in_specs=[
            pl.BlockSpec((seq, aw), lambda b: (b, 0)),
            pl.BlockSpec((seq, kvw), lambda b: (b, 0)),
            pl.BlockSpec((seq, kvw), lambda b: (b, 0)),
            pl.BlockSpec(memory_space=pl.ANY),
        ],
        out_specs=pl.BlockSpec((seq, aw), lambda b: (b, 0)),
        out_shape=jax.ShapeDtypeStruct(mix.shape, mix.dtype),
        input_output_aliases={3: 0},
        compiler_params=_cp("arbitrary"),
        name="attention_context",
    )(q, k, v, mix)

    assert t_ctx % dec_seq == 0
    n_lat = (t - t_ctx) // dec_seq
    tq = _tile(dec_seq, 512)
    nqb = dec_seq // tq
    q_off = t_ctx // tq
    k_off = t_ctx // dec_seq
    qw = qpk * hd
    return pl.pallas_call(
        functools.partial(_attn_lat_kernel, qpk=qpk, hd=hd),
        grid=(n_lat, nkv, nqb),
        in_specs=[
            pl.BlockSpec((tq, qw), lambda b, kv, qi: (q_off + b * nqb + qi, kv)),
            pl.BlockSpec((dec_seq, hd), lambda b, kv, qi: (k_off + b, kv)),
            pl.BlockSpec((dec_seq, hd), lambda b, kv, qi: (k_off + b, kv)),
            pl.BlockSpec((None, None, past, hd), lambda b, kv, qi: (b, layer, 0, kv)),
            pl.BlockSpec((None, None, past, hd), lambda b, kv, qi: (b, layer, 0, kv)),
            pl.BlockSpec(memory_space=pl.ANY),
        ],
        out_specs=pl.BlockSpec((tq, qw), lambda b, kv, qi: (q_off + b * nqb + qi, kv)),
        out_shape=jax.ShapeDtypeStruct(mix.shape, mix.dtype),
        input_output_aliases={5: 0},
        compiler_params=_cp("arbitrary", "arbitrary", "arbitrary"),
        name="attention_latent",
    )(q, k, v, cache_k4, cache_v4, mix)


def _fourier_kernel(u_ref, dpos_ref, dch_ref, wf_ref, mix_ref, o_ref, tt_ref, *, ng, gd, norm):
    s_len = u_ref.shape[0]
    for g in range(ng):
        sl = slice(g * gd, (g + 1) * gd)
        t = jnp.dot(u_ref[:, sl], dch_ref[...], preferred_element_type=F32)
        tt_ref[0:s_len, sl] = t[:, :gd].astype(BF16)
        tt_ref[s_len:2 * s_len, sl] = t[:, gd:].astype(BF16)
    f = (jnp.dot(dpos_ref[...], tt_ref[...], preferred_element_type=F32) * norm).astype(BF16)
    for g in range(ng):
        sl = slice(g * gd, (g + 1) * gd)
        o_ref[:, sl] = jnp.dot(f[:, sl], wf_ref[g].astype(BF16),
                               preferred_element_type=F32).astype(o_ref.dtype)


def _dft_tables(n):
    jk = (jnp.arange(n, dtype=jnp.int32)[:, None] * jnp.arange(n, dtype=jnp.int32)[None, :]) % n
    ang = jk.astype(F32) * (2.0 * jnp.pi / n)
    return jnp.cos(ang), jnp.sin(ang)


def _fourier(u, w_fourier, mix, layer, s_len, row_off, n_seq, tables):
    ng, gd = w_fourier.shape[1], w_fourier.shape[2]
    fw = ng * gd
    assert (mix.shape[1] - fw) % fw == 0 and row_off % s_len == 0
    col = (mix.shape[1] - fw) // fw
    dpos, dch = tables
    off = row_off // s_len
    return pl.pallas_call(
        functools.partial(_fourier_kernel, ng=ng, gd=gd, norm=float(s_len * gd) ** -0.5),
        grid=(n_seq,),
        in_specs=[
            pl.BlockSpec((s_len, fw), lambda b: (off + b, 0)),
            pl.BlockSpec((s_len, 2 * s_len), lambda b: (0, 0)),
            pl.BlockSpec((gd, 2 * gd), lambda b: (0, 0)),
            pl.BlockSpec((None, ng, gd, gd), lambda b: (layer, 0, 0, 0)),
            pl.BlockSpec(memory_space=pl.ANY),
        ],
        out_specs=pl.BlockSpec((s_len, fw), lambda b: (off + b, col)),
        out_shape=jax.ShapeDtypeStruct(mix.shape, mix.dtype),
        input_output_aliases={4: 0},
        scratch_shapes=[pltpu.VMEM((2 * s_len, fw), BF16)],
        compiler_params=_cp("arbitrary"),
        name="fourier_mix",
    )(u, dpos, dch, w_fourier, mix)


def _wout_kernel(m_ref, w_ref, x_ref, g_ref, o_ref):
    acc = jnp.dot(m_ref[...], w_ref[...].astype(BF16), preferred_element_type=F32)
    o_ref[...] = x_ref[...] + g_ref[...] * acc


def _out_project(mix, w_out, x, mods, layer, dims):
    t, d = x.shape
    mw = mix.shape[1]
    rows = _Rows(dims["t_ctx"], dims["dec_seq"], _tile(min(dims["t_ctx"], dims["dec_seq"]), 1024))
    tm = rows.tm
    tn = _tile(d, 512)
    return pl.pallas_call(
        _wout_kernel,
        grid=(t // tm, d // tn),
        in_specs=[
            pl.BlockSpec((tm, mw), lambda i, j: (i, 0)),
            pl.BlockSpec((None, mw, tn), lambda i, j: (layer, 0, j)),
            pl.BlockSpec((tm, tn), lambda i, j: (i, j)),
            _mod_spec(rows, layer, 2, tn, col=lambda j: j),
        ],
        out_specs=pl.BlockSpec((tm, tn), lambda i, j: (i, j)),
        out_shape=jax.ShapeDtypeStruct((t, d), F32),
        compiler_params=_cp("arbitrary", "arbitrary"),
        name="out_projection",
    )(mix, w_out, x, mods)


def _top2_sum(a, b, c, d):
    hi1, lo1 = jnp.maximum(a, b), jnp.minimum(a, b)
    hi2, lo2 = jnp.maximum(c, d), jnp.minimum(c, d)
    return jnp.maximum(hi1, hi2) + jnp.maximum(jnp.minimum(hi1, hi2), jnp.maximum(lo1, lo2))


def _norm_router_kernel(x_ref, g_ref, sc_ref, sh_ref, wr_ref, rb_ref, hp_ref, idx_ref, wt_ref,
                        *, n_exp, n_grp):
    hf = _norm_mod(x_ref[...], g_ref[...], sc_ref[...], sh_ref[...])
    half = hf.shape[1] // 2
    hp_ref[...] = _pack_pair(hf[:, :half], hf[:, half:])

    hi = hf.astype(BF16)
    lo = (hf - hi.astype(F32)).astype(BF16)
    w = wr_ref[...]
    whi = w.astype(BF16)
    wlo = (w - whi.astype(F32)).astype(BF16)
    lg = (lax.dot_general(whi, hi, _NT, preferred_element_type=F32)
          + lax.dot_general(whi, lo, _NT, preferred_element_type=F32)
          + lax.dot_general(wlo, hi, _NT, preferred_element_type=F32))
    ex = jnp.exp(lg - lg.max(axis=0, keepdims=True))
    aff = ex * (1.0 / ex.sum(axis=0, keepdims=True))
    sel = aff + rb_ref[...]

    per = n_exp // n_grp
    row = lambda a, e: a[e:e + 1, :]
    assert per == 4, "group score uses a 4-element top-2 network"
    best_s = _top2_sum(*[row(sel, e) for e in range(per)])
    best_g = jnp.zeros_like(best_s, dtype=jnp.int32)
    for g in range(1, n_grp):
        s = _top2_sum(*[row(sel, g * per + e) for e in range(per)])
        upd = s > best_s
        best_g = jnp.where(upd, g, best_g)
        best_s = jnp.where(upd, s, best_s)

    neg = jnp.float32(-jnp.inf)
    picks = []
    for _ in range(2):
        bv = jnp.full_like(best_s, neg)
        bi = jnp.zeros_like(best_g)
        ba = jnp.zeros_like(best_s)
        for e in range(n_exp):
            ok = best_g == (e // per)
            for (pi, _) in picks:
                ok = jnp.logical_and(ok, pi != e)
            val = jnp.where(ok, row(sel, e), neg)
            upd = val > bv
            bv = jnp.where(upd, val, bv)
            bi = jnp.where(upd, e, bi)
            ba = jnp.where(upd, row(aff, e), ba)
        picks.append((bi, ba))
    (i0, a0), (i1, a1) = picks
    inv = 1.0 / (a0 + a1)
    idx_ref[0:1, :] = i0
    idx_ref[1:2, :] = i1
    wt_ref[0:1, :] = a0 * inv
    wt_ref[1:2, :] = a1 * inv


def _norm_router(x, gain, mods, w_router_t, router_bias, layer, dims):
    t, d = x.shape
    n_exp = w_router_t.shape[0]
    rows = _Rows(dims["t_ctx"], dims["dec_seq"], _tile(min(dims["t_ctx"], dims["dec_seq"]), 256))
    tm = rows.tm
    return pl.pallas_call(
        functools.partial(_norm_router_kernel, n_exp=n_exp, n_grp=N_EXPERT_GROUPS),
        grid=(t // tm,),
        in_specs=[
            pl.BlockSpec((tm, d), lambda i: (i, 0)),
            pl.BlockSpec((None, 1, d), lambda i: (layer, 0, 0)),
            _mod_spec(rows, layer, 4, d),
            _mod_spec(rows, layer, 3, d),
            pl.BlockSpec((n_exp, d), lambda i: (0, 0)),
            pl.BlockSpec((n_exp, 1), lambda i: (0, 0)),
        ],
        out_specs=[
            pl.BlockSpec((tm, d // 2), lambda i: (i, 0)),
            pl.BlockSpec((2, tm), lambda i: (0, i)),
            pl.BlockSpec((2, tm), lambda i: (0, i)),
        ],
        out_shape=[
            jax.ShapeDtypeStruct((t, d // 2), U32),
            jax.ShapeDtypeStruct((2, t), jnp.int32),
            jax.ShapeDtypeStruct((2, t), F32),
        ],
        compiler_params=_cp("arbitrary"),
        name="norm_router",
    )(x, gain, mods, mods, w_router_t, router_bias.reshape(n_exp, 1))


def _dispatch_plan(idx, n_exp, item_rows, sub_rows, n_items):
    t = idx.shape[1]
    e_a = idx.reshape(-1)
    onehot = (e_a[:, None] == jnp.arange(n_exp, dtype=jnp.int32)[None, :]).astype(jnp.int32)
    csum = jnp.cumsum(onehot, axis=0)
    counts = csum[-1]
    rank = jnp.take_along_axis(csum, e_a[:, None], axis=1)[:, 0] - 1
    items_e = (counts + item_rows - 1) // item_rows
    item_end = jnp.cumsum(items_e)
    item_start = item_end - items_e
    pos = item_start[e_a] * item_rows + rank
    tok = jnp.tile(jnp.arange(t, dtype=jnp.int32), 2)
    slot_tok = jnp.zeros((n_items * item_rows,), jnp.int32).at[pos].set(tok)
    w = jnp.arange(n_items, dtype=jnp.int32)
    item_exp = jnp.minimum(jnp.searchsorted(item_end, w, side="right"), n_exp - 1).astype(jnp.int32)
    rows = jnp.clip(counts[item_exp] - (w - item_start[item_exp]) * item_rows, 0, item_rows)
    n_sub = jnp.where(w < item_end[-1], (rows + sub_rows - 1) // sub_rows, 0).astype(jnp.int32)
    n_valid = item_end[-1].astype(jnp.int32).reshape(1)
    return slot_tok, pos.reshape(2, t), item_exp, n_sub, n_valid


def _row_copy(src_hbm, dst, sem, src_row, dst_row, n=1):
    return pltpu.make_async_copy(src_hbm.at[pl.ds(src_row, n)], dst.at[pl.ds(dst_row, n)], sem)


GATHER_UNROLL = 8


def _start_row_gather(src_hbm, dst, sem, idx_ref, n_rows):
    def issue(g, carry):
        base = g * GATHER_UNROLL
        for u in range(GATHER_UNROLL):
            _row_copy(src_hbm, dst, sem, idx_ref[0, 0, base + u], base + u).start()
        return carry

    lax.fori_loop(0, n_rows // GATHER_UNROLL, issue, 0)


def _zero_tail(o_ref, first_sub, n_sub_total, sub):
    def body(s, carry):
        o_ref[pl.ds(pl.multiple_of(s * sub, sub), sub), :] = jnp.zeros((sub, o_ref.shape[1]), o_ref.dtype)
        return carry

    lax.fori_loop(first_sub, n_sub_total, body, 0)


def _ffn_up_kernel(te_ref, ns_ref, nv_ref, tok_ref, tok_next_ref, hp_ref, w1_ref, w3_ref, o_ref,
                   xbuf, xl_ref, xr_ref, wb1_ref, wb3_ref, sem, *, sub):
    w = pl.program_id(0)
    c = pl.program_id(1)
    nv = nv_ref[0]
    n_sub_total = o_ref.shape[0] // sub
    half = xl_ref.shape[1]

    def start_gather(toks, n_sub):
        _start_row_gather(hp_ref, xbuf, sem, toks, n_sub * sub)

    @pl.when(jnp.logical_and(w == 0, c == 0))
    def _():
        start_gather(tok_ref, ns_ref[0])

    @pl.when(jnp.logical_and(c == 0, w < nv))
    def _():
        def land(s, carry):
            _row_copy(hp_ref, xbuf, sem, 0, 0, sub).wait()
            return carry

        lax.fori_loop(0, ns_ref[w], land, 0)

        def unpack(s, carry):
            rows = pl.ds(pl.multiple_of(s * sub, sub), sub)
            lo, hi = _unpack_pair(xbuf[rows, :])
            xl_ref[rows, :] = lo.astype(BF16)
            xr_ref[rows, :] = hi.astype(BF16)
            return carry

        lax.fori_loop(0, ns_ref[w], unpack, 0)

        @pl.when(w + 1 < nv)
        def _():
            start_gather(tok_next_ref, ns_ref[jnp.minimum(w + 1, ns_ref.shape[0] - 1)])

    @pl.when(w < nv)
    def _():
        wb1_ref[...] = w1_ref[...].astype(BF16)
        wb3_ref[...] = w3_ref[...].astype(BF16)

        def body(s, carry):
            rows = pl.ds(pl.multiple_of(s * sub, sub), sub)
            xl = xl_ref[rows, :]
            xr = xr_ref[rows, :]
            a = (jnp.dot(xl, wb1_ref[:half, :], preferred_element_type=F32)
                 + jnp.dot(xr, wb1_ref[half:, :], preferred_element_type=F32))
            b = (jnp.dot(xl, wb3_ref[:half, :], preferred_element_type=F32)
                 + jnp.dot(xr, wb3_ref[half:, :], preferred_element_type=F32))
            o_ref[rows, :] = (a * jax.nn.sigmoid(a) * b).astype(o_ref.dtype)
            return carry

        lax.fori_loop(0, ns_ref[w], body, 0)
        _zero_tail(o_ref, ns_ref[w], n_sub_total, sub)

    @pl.when(w >= nv)
    def _():
        o_ref[...] = jnp.zeros_like(o_ref)


def _ffn_down_kernel(te_ref, ns_ref, nv_ref, h_ref, wl_ref, wr_ref, o_ref, wbl_ref, wbr_ref, *, sub):
    w = pl.program_id(0)
    n_sub_total = o_ref.shape[0] // sub

    @pl.when(w < nv_ref[0])
    def _():
        wbl_ref[...] = wl_ref[...].astype(BF16)
        wbr_ref[...] = wr_ref[...].astype(BF16)

        def body(s, carry):
            rows = pl.ds(pl.multiple_of(s * sub, sub), sub)
            h = h_ref[rows, :]
            yl = jnp.dot(h, wbl_ref[...], preferred_element_type=F32)
            yr = jnp.dot(h, wbr_ref[...], preferred_element_type=F32)
            o_ref[rows, :] = _pack_pair(yl, yr)
            return carry

        lax.fori_loop(0, ns_ref[w], body, 0)
        _zero_tail(o_ref, ns_ref[w], n_sub_total, sub)

    @pl.when(w >= nv_ref[0])
    def _():
        o_ref[...] = jnp.zeros_like(o_ref)


def _expert_ffn(hp, slot_tok, item_exp, n_sub, n_valid, w1, w3, w2, layer, item_rows, sub_rows):
    t, dh = hp.shape
    d = 2 * dh
    f = w1.shape[3]
    n_items = item_exp.shape[0]
    r_total = n_items * item_rows
    tf = _tile(f, 256)
    nf = f // tf
    toks = slot_tok.reshape(n_items, 1, item_rows)

    def live(w, nv):
        return jnp.minimum(w, nv[0] - 1)

    def chunk(w, c, nv, n_chunks):
        return jnp.where(w < nv[0], c, n_chunks - 1)

    hid = pl.pallas_call(
        functools.partial(_ffn_up_kernel, sub=sub_rows),
        grid_spec=pltpu.PrefetchScalarGridSpec(
            num_scalar_prefetch=3,
            grid=(n_items, nf),
            in_specs=[
                pl.BlockSpec((1, 1, item_rows), lambda w, c, te, ns, nv: (w, 0, 0),
                             memory_space=pltpu.SMEM),
                pl.BlockSpec((1, 1, item_rows),
                             lambda w, c, te, ns, nv: (jnp.minimum(w + 1, n_items - 1), 0, 0),
                             memory_space=pltpu.SMEM),
                pl.BlockSpec(memory_space=pl.ANY),
                pl.BlockSpec((None, None, d, tf),
                             lambda w, c, te, ns, nv: (layer, te[live(w, nv)], 0, chunk(w, c, nv, nf))),
                pl.BlockSpec((None, None, d, tf),
                             lambda w, c, te, ns, nv: (layer, te[live(w, nv)], 0, chunk(w, c, nv, nf))),
            ],
            out_specs=pl.BlockSpec((item_rows, tf), lambda w, c, te, ns, nv: (w, c)),
            scratch_shapes=[
                pltpu.VMEM((item_rows, dh), U32),
                pltpu.VMEM((item_rows, dh), BF16),
                pltpu.VMEM((item_rows, dh), BF16),
                pltpu.VMEM((d, tf), BF16),
                pltpu.VMEM((d, tf), BF16),
                pltpu.SemaphoreType.DMA,
            ],
        ),
        out_shape=jax.ShapeDtypeStruct((r_total, f), BF16),
        compiler_params=_cp("arbitrary", "arbitrary"),
        name="expert_ffn_up",
    )(item_exp, n_sub, n_valid, toks, toks, hp, w1, w3)

    tn = _tile(dh, 512)
    nn = dh // tn
    return pl.pallas_call(
        functools.partial(_ffn_down_kernel, sub=sub_rows),
        grid_spec=pltpu.PrefetchScalarGridSpec(
            num_scalar_prefetch=3,
            grid=(n_items, nn),
            in_specs=[
                pl.BlockSpec((item_rows, f), lambda w, c, te, ns, nv: (live(w, nv), 0)),
                pl.BlockSpec((None, None, f, tn),
                             lambda w, c, te, ns, nv: (layer, te[live(w, nv)], 0, chunk(w, c, nv, nn))),
                pl.BlockSpec((None, None, f, tn),
                             lambda w, c, te, ns, nv: (layer, te[live(w, nv)], 0,
                                                       nn + chunk(w, c, nv, nn))),
            ],
            out_specs=pl.BlockSpec((item_rows, tn), lambda w, c, te, ns, nv: (w, c)),
            scratch_shapes=[pltpu.VMEM((f, tn), BF16), pltpu.VMEM((f, tn), BF16)],
        ),
        out_shape=jax.ShapeDtypeStruct((r_total, dh), U32),
        compiler_params=_cp("arbitrary", "arbitrary"),
        name="expert_ffn_down",
    )(item_exp, n_sub, n_valid, hid, w2, w2)


def _combine_kernel(pos_ref, pos_next_ref, y_ref, x_ref, g_ref, wt_ref, *rest, tm, with_next_norm):
    if with_next_norm:
        gn_ref, scn_ref, shn_ref, o_ref, h_ref, ybuf, sem = rest
    else:
        o_ref, ybuf, sem = rest
    i = pl.program_id(0)
    slot = i % 2

    def start_gather(poss, dst_slot):
        _start_row_gather(y_ref, ybuf.at[dst_slot], sem.at[dst_slot], poss, 2 * tm)

    @pl.when(i == 0)
    def _():
        start_gather(pos_ref, 0)

    @pl.when(i + 1 < pl.num_programs(0))
    def _():
        start_gather(pos_next_ref, 1 - slot)

    _row_copy(y_ref, ybuf.at[slot], sem.at[slot], 0, 0, 2 * tm).wait()

    half = ybuf.shape[2]
    w0 = wt_ref[:, 0:1]
    w1 = wt_ref[:, 1:2]
    l0, r0 = _unpack_pair(ybuf[slot, 0:tm, :])
    l1, r1 = _unpack_pair(ybuf[slot, tm:2 * tm, :])
    ol = x_ref[:, :half] + g_ref[:, :half] * (w0 * l0 + w1 * l1)
    orr = x_ref[:, half:] + g_ref[:, half:] * (w0 * r0 + w1 * r1)
    o_ref[:, :half] = ol
    o_ref[:, half:] = orr
    if with_next_norm:
        ms = (jnp.sum(ol * ol, axis=-1, keepdims=True)
              + jnp.sum(orr * orr, axis=-1, keepdims=True)) * (1.0 / (2 * half))
        inv = lax.rsqrt(ms + EPS)
        for sl, part in ((slice(0, half), ol), (slice(half, 2 * half), orr)):
            yn = part * inv * gn_ref[:, sl]
            h_ref[:, sl] = (yn * (1.0 + scn_ref[:, sl]) + shn_ref[:, sl]).astype(h_ref.dtype)


def _combine(y, pos, wts, x, mods, layer, dims, next_gain=None):
    t, d = x.shape
    rows = _Rows(dims["t_ctx"], dims["dec_seq"], _tile(min(dims["t_ctx"], dims["dec_seq"]), 256))
    tm = rows.tm
    nt = t // tm
    with_next = next_gain is not None
    pos_tiles = pos.reshape(2, nt, tm).transpose(1, 0, 2).reshape(nt, 1, 2 * tm)
    row_spec = pl.BlockSpec((tm, d), lambda i: (i, 0))
    in_specs = [
        pl.BlockSpec((1, 1, 2 * tm), lambda i: (i, 0, 0), memory_space=pltpu.SMEM),
        pl.BlockSpec((1, 1, 2 * tm), lambda i: (jnp.minimum(i + 1, nt - 1), 0, 0),
                     memory_space=pltpu.SMEM),
        pl.BlockSpec(memory_space=pl.ANY),
        row_spec,
        _mod_spec(rows, layer, 5, d),
        pl.BlockSpec((tm, 2), lambda i: (i, 0)),
    ]
    args = [pos_tiles, pos_tiles, y, x, mods, wts.T]
    out_specs, out_shape = row_spec, jax.ShapeDtypeStruct((t, d), F32)
    if with_next:
        in_specs += [pl.BlockSpec((None, 1, d), lambda i: (layer + 1, 0, 0)),
                     _mod_spec(rows, layer + 1, 1, d), _mod_spec(rows, layer + 1, 0, d)]
        args += [next_gain, mods, mods]
        out_specs = [row_spec, row_spec]
        out_shape = [out_shape, jax.ShapeDtypeStruct((t, d), BF16)]
    return pl.pallas_call(
        functools.partial(_combine_kernel, tm=tm, with_next_norm=with_next),
        grid=(nt,),
        in_specs=in_specs,
        out_specs=out_specs,
        out_shape=out_shape,
        scratch_shapes=[pltpu.VMEM((2, 2 * tm, d // 2), U32), pltpu.SemaphoreType.DMA((2,))],
        compiler_params=_cp("arbitrary"),
        name="moe_combine",
    )(*args)


MOE_ITEM_ROWS = 1536
MOE_SUB_ROWS = 256


def kernel(x_prompt, x_sample, cache_k, cache_v, c, c_ctx, w_ada, b_ada, norm_mix, norm_ffn, w_in, q_norm, k_norm, w_fourier, w_out, w_router, router_bias, w1, w3, w2):
    batch, seq, d = x_prompt.shape
    dec_batch, dec_seq, _ = x_sample.shape
    depth = w_in.shape[0]
    past, nkv, hd = cache_k.shape[2], cache_k.shape[3], cache_k.shape[4]
    ng, gd = w_fourier.shape[1], w_fourier.shape[2]
    fw = ng * gd
    kvw = nkv * hd
    aw = w_out.shape[1] - fw
    n_exp = w_router.shape[1]
    t_ctx = batch * seq
    t_lat = dec_batch * dec_seq
    t = t_ctx + t_lat
    assert hd == LANE and dec_batch + 1 <= COND_ROWS and w_in.shape[2] == aw + 2 * kvw + fw
    dims = dict(hd=hd, aw=aw, kvw=kvw, fw=fw, nkv=nkv, qpk=aw // kvw, seq=seq, dec_seq=dec_seq,
                t_ctx=t_ctx, past=past)

    cond = jnp.zeros((COND_ROWS, d), F32).at[0].set(c_ctx).at[1:1 + dec_batch].set(c)
    mods = _modulation(cond, w_ada, b_ada).reshape(depth, COND_ROWS, N_MOD, 1, d)

    rope = _rope_tables(dec_seq, hd)
    cs_c, sn_c = _dft_tables(seq)
    cs_l, sn_l = _dft_tables(dec_seq)
    cs_g, sn_g = _dft_tables(gd)
    dch = jnp.concatenate([cs_g, sn_g], axis=1).astype(BF16)
    dft_ctx = (jnp.concatenate([cs_c, -sn_c], axis=1).astype(BF16), dch)
    dft_lat = (jnp.concatenate([cs_l, -sn_l], axis=1).astype(BF16), dch)

    cache_k4 = cache_k.reshape(dec_batch, depth, past, kvw)
    cache_v4 = cache_v.reshape(dec_batch, depth, past, kvw)
    w_router_t = w_router.T
    n_moe_items = (2 * t) // MOE_ITEM_ROWS + n_exp
    mix = jnp.zeros((t, aw + fw), BF16)
    norm_mix3 = norm_mix.reshape(depth, 1, d)
    norm_ffn3 = norm_ffn.reshape(depth, 1, d)
    q_norm3 = q_norm.reshape(depth, 1, hd)
    k_norm3 = k_norm.reshape(depth, 1, hd)

    x = jnp.concatenate([x_prompt.reshape(t_ctx, d), x_sample.reshape(t_lat, d)], axis=0)
    new_k = jnp.zeros((batch, depth, seq, kvw), F32)
    new_v = jnp.zeros((batch, depth, seq, kvw), F32)
    h = _norm_modulate(x, norm_mix3, mods, 0, t_ctx, dec_seq)
    for l in range(depth):
        q, k, v, u = _project(h, w_in, q_norm3, k_norm3, rope, l, dims)
        new_k, new_v = _store_context_cache(k, v, new_k, new_v, l, dims)
        mix = _attention(q, k, v, cache_k4, cache_v4, mix, l, dims)
        mix = _fourier(u, w_fourier, mix, l, seq, 0, batch, dft_ctx)
        mix = _fourier(u, w_fourier, mix, l, dec_seq, t_ctx, dec_batch, dft_lat)
        x = _out_project(mix, w_out, x, mods, l, dims)

        hp, idx, wts = _norm_router(x, norm_ffn3, mods, w_router_t, router_bias, l, dims)
        slot_tok, pos, item_exp, n_sub, n_valid = _dispatch_plan(
            idx, n_exp, MOE_ITEM_ROWS, MOE_SUB_ROWS, n_moe_items)
        y = _expert_ffn(hp, slot_tok, item_exp, n_sub, n_valid, w1, w3, w2, l,
                        MOE_ITEM_ROWS, MOE_SUB_ROWS)
        if l + 1 < depth:
            x, h = _combine(y, pos, wts, x, mods, l, dims, next_gain=norm_mix3)
        else:
            x = _combine(y, pos, wts, x, mods, l, dims)

    y_prompt = x[:t_ctx].reshape(batch, seq, d)
    y_sample = x[t_ctx:].reshape(dec_batch, dec_seq, d)
    cache_shape = (batch, depth, seq, nkv, hd)
    return (y_prompt, y_sample, new_k.reshape(cache_shape), new_v.reshape(cache_shape))
```

```python
import functools
import math

import jax
import jax.numpy as jnp
import numpy as np
from jax import lax
from jax.experimental import pallas as pl
from jax.experimental.pallas import tpu as pltpu

GRID_W = 64
N_EXPERT_GROUPS = 4
ROPE_THETA = 10000.0
EPS = 1e-6
N_MOD = 6
COND_ROWS = 8

VMEM_LIMIT_BYTES = 56 * 1024 * 1024
LANE = 128

BF16 = jnp.bfloat16
F32 = jnp.float32
U32 = jnp.uint32

_NT = (((1,), (1,)), ((), ()))


def _cp(*sem):
    return pltpu.CompilerParams(dimension_semantics=sem, vmem_limit_bytes=VMEM_LIMIT_BYTES)


def _tile(n, pref):
    t = min(n, pref)
    while n % t:
        t -= 1
    return t


def _pack_pair(a, b):
    ua = lax.bitcast_convert_type(a.astype(BF16).astype(F32), U32)
    ub = lax.bitcast_convert_type(b.astype(BF16).astype(F32), U32)
    return (ua >> 16) | (ub & jnp.uint32(0xFFFF0000))


def _unpack_pair(w):
    lo = lax.bitcast_convert_type(w << 16, F32)
    hi = lax.bitcast_convert_type(w & jnp.uint32(0xFFFF0000), F32)
    return lo, hi


def _mod_kernel(c_ref, w_ref, b_ref, o_ref):
    c = c_ref[...]
    s = (c * jax.nn.sigmoid(c)).astype(BF16)
    o_ref[...] = jnp.dot(s, w_ref[...].astype(BF16), preferred_element_type=F32) + b_ref[...]


def _modulation(cond, w_ada, b_ada):
    depth, d, n = w_ada.shape
    tn = _tile(n, 512)
    return pl.pallas_call(
        _mod_kernel,
        grid=(depth, n // tn),
        in_specs=[
            pl.BlockSpec((COND_ROWS, d), lambda l, j: (0, 0)),
            pl.BlockSpec((None, d, tn), lambda l, j: (l, 0, j)),
            pl.BlockSpec((None, 1, tn), lambda l, j: (l, 0, j)),
        ],
        out_specs=pl.BlockSpec((None, COND_ROWS, tn), lambda l, j: (l, 0, j)),
        out_shape=jax.ShapeDtypeStruct((depth, COND_ROWS, n), F32),
        compiler_params=_cp("arbitrary", "arbitrary"),
        name="modulation",
    )(cond, w_ada, b_ada.reshape(depth, 1, n))


class _Rows:
    def __init__(self, t_ctx, dec_seq, tm):
        assert t_ctx % tm == 0 and dec_seq % tm == 0
        self.tm = tm
        self.n_ctx_tiles = t_ctx // tm
        self.tiles_per_seq = dec_seq // tm

    def cond(self, i):
        lat = 1 + (i - self.n_ctx_tiles) // self.tiles_per_seq
        return jnp.where(i < self.n_ctx_tiles, 0, lat)

    def seq_tile(self, i):
        return jnp.maximum(i - self.n_ctx_tiles, 0) % self.tiles_per_seq


def _mod_spec(rows, layer, which, width, col=None):
    if col is None:
        return pl.BlockSpec((None, None, None, 1, width),
                            lambda i, *_: (layer, rows.cond(i), which, 0, 0))
    return pl.BlockSpec((None, None, None, 1, width),
                        lambda i, j, *_: (layer, rows.cond(i), which, 0, col(j)))


def _norm_mod(x, g, sc, sh):
    ms = jnp.mean(x * x, axis=-1, keepdims=True)
    y = x * lax.rsqrt(ms + EPS) * g
    return y * (1.0 + sc) + sh


def _norm_mod_kernel(x_ref, g_ref, sc_ref, sh_ref, o_ref):
    o_ref[...] = _norm_mod(x_ref[...], g_ref[...], sc_ref[...], sh_ref[...]).astype(o_ref.dtype)


def _norm_modulate(x, gain, mods, layer, t_ctx, dec_seq):
    t, d = x.shape
    rows = _Rows(t_ctx, dec_seq, _tile(min(t_ctx, dec_seq), 256))
    tm = rows.tm
    return pl.pallas_call(
        _norm_mod_kernel,
        grid=(t // tm,),
        in_specs=[
            pl.BlockSpec((tm, d), lambda i: (i, 0)),
            pl.BlockSpec((None, 1, d), lambda i: (layer, 0, 0)),
            _mod_spec(rows, layer, 1, d),
            _mod_spec(rows, layer, 0, d),
        ],
        out_specs=pl.BlockSpec((tm, d), lambda i: (i, 0)),
        out_shape=jax.ShapeDtypeStruct((t, d), BF16),
        compiler_params=_cp("arbitrary"),
        name="norm_modulate",
    )(x, gain, mods, mods)


def _proj_kernel(h_ref, w_ref, qg_ref, kg_ref, cos_ref, se_ref, so_ref,
                 q_ref, k_ref, v_ref, u_ref, *, nq, nk, nv, n_ctx_tiles, hd, q_scale):
    i = pl.program_id(0)
    j = pl.program_id(1)
    acc = jnp.dot(h_ref[...], w_ref[...].astype(BF16), preferred_element_type=F32)
    heads = acc.shape[1] // hd

    def head_norm(gain_ref, out_ref, mult, rope):
        for hh in range(heads):
            sl = slice(hh * hd, (hh + 1) * hd)
            blk = acc[:, sl]
            ms = jnp.mean(blk * blk, axis=-1, keepdims=True)
            y = blk * lax.rsqrt(ms + EPS) * gain_ref[...]
            if rope:
                y = (y * cos_ref[...] + pltpu.roll(y, hd - 1, 1) * se_ref[...]
                     + pltpu.roll(y, 1, 1) * so_ref[...])
            if mult != 1.0:
                y = y * mult
            out_ref[:, sl] = y.astype(out_ref.dtype)

    def qk_path(gain_ref, out_ref, mult):
        @pl.when(i < n_ctx_tiles)
        def _():
            head_norm(gain_ref, out_ref, mult, False)

        @pl.when(i >= n_ctx_tiles)
        def _():
            head_norm(gain_ref, out_ref, mult, True)

    @pl.when(j < nq)
    def _():
        qk_path(qg_ref, q_ref, q_scale)

    @pl.when(jnp.logical_and(j >= nq, j < nq + nk))
    def _():
        qk_path(kg_ref, k_ref, 1.0)

    @pl.when(jnp.logical_and(j >= nq + nk, j < nq + nk + nv))
    def _():
        v_ref[...] = acc

    @pl.when(j >= nq + nk + nv)
    def _():
        u_ref[...] = acc.astype(u_ref.dtype)


def _rope_tables(n_tok, hd):
    axis_dim = hd // 2
    rows = n_tok // GRID_W
    row_idx = np.repeat(np.arange(rows, dtype=np.float64), GRID_W)
    col_idx = np.tile(np.arange(GRID_W, dtype=np.float64), rows)
    freqs = ROPE_THETA ** (-np.arange(0, axis_dim, 2, dtype=np.float64) / axis_dim)
    ang = np.concatenate([row_idx[:, None] * freqs, col_idx[:, None] * freqs], axis=-1)
    cos = np.repeat(np.cos(ang), 2, axis=-1)
    sin = np.repeat(np.sin(ang), 2, axis=-1)
    even = (np.arange(hd) % 2 == 0)[None, :]
    as_const = lambda a: jnp.asarray(a.astype(np.float32))
    return as_const(cos), as_const(np.where(even, -sin, 0.0)), as_const(np.where(even, 0.0, sin))


def _project(h, w_in, q_g, k_g, rope, layer, dims):
    t, d = h.shape
    hd, aw, kvw, fw = dims["hd"], dims["aw"], dims["kvw"], dims["fw"]
    tn = _tile(math.gcd(math.gcd(aw, kvw), fw), 512)
    rows = _Rows(dims["t_ctx"], dims["dec_seq"], _tile(min(dims["t_ctx"], dims["dec_seq"]), 1024))
    tm = rows.tm
    nq, nk, nv, nu = aw // tn, kvw // tn, kvw // tn, fw // tn
    cos, se, so = rope
    kern = functools.partial(_proj_kernel, nq=nq, nk=nk, nv=nv, n_ctx_tiles=rows.n_ctx_tiles,
                             hd=hd, q_scale=float(hd) ** -0.5)

    def clampj(lo, n):
        return lambda i, j: (i, jnp.clip(j - lo, 0, n - 1))

    rope_spec = pl.BlockSpec((tm, hd), lambda i, j: (rows.seq_tile(i), 0))
    return pl.pallas_call(
        kern,
        grid=(t // tm, nq + nk + nv + nu),
        in_specs=[
            pl.BlockSpec((tm, d), lambda i, j: (i, 0)),
            pl.BlockSpec((None, d, tn), lambda i, j: (layer, 0, j)),
            pl.BlockSpec((None, 1, hd), lambda i, j: (layer, 0, 0)),
            pl.BlockSpec((None, 1, hd), lambda i, j: (layer, 0, 0)),
            rope_spec, rope_spec, rope_spec,
        ],
        out_specs=[
            pl.BlockSpec((tm, tn), clampj(0, nq)),
            pl.BlockSpec((tm, tn), clampj(nq, nk)),
            pl.BlockSpec((tm, tn), clampj(nq + nk, nv)),
            pl.BlockSpec((tm, tn), clampj(nq + nk + nv, nu)),
        ],
        out_shape=[
            jax.ShapeDtypeStruct((t, aw), BF16),
            jax.ShapeDtypeStruct((t, kvw), F32),
            jax.ShapeDtypeStruct((t, kvw), F32),
            jax.ShapeDtypeStruct((t, fw), BF16),
        ],
        compiler_params=_cp("arbitrary", "arbitrary"),
        name="in_projection",
    )(h, w_in, q_g, k_g, cos, se, so)


def _cache_store_kernel(k_ref, v_ref, kc_in_ref, vc_in_ref, kc_ref, vc_ref):
    kc_ref[...] = k_ref[...].reshape(kc_ref.shape)
    vc_ref[...] = v_ref[...].reshape(vc_ref.shape)


def _store_context_cache(k, v, kc, vc, layer, dims):
    seq, t_ctx = dims["seq"], dims["t_ctx"]
    kvw = k.shape[1]
    tm = _tile(t_ctx // seq, 4) * seq
    nb = tm // seq
    out_spec = pl.BlockSpec((nb, None, seq, kvw), lambda i: (i, layer, 0, 0))
    return pl.pallas_call(
        _cache_store_kernel,
        grid=(t_ctx // tm,),
        in_specs=[
            pl.BlockSpec((tm, kvw), lambda i: (i, 0)),
            pl.BlockSpec((tm, kvw), lambda i: (i, 0)),
            pl.BlockSpec(memory_space=pl.ANY),
            pl.BlockSpec(memory_space=pl.ANY),
        ],
        out_specs=[out_spec, out_spec],
        out_shape=[jax.ShapeDtypeStruct(kc.shape, kc.dtype), jax.ShapeDtypeStruct(vc.shape, vc.dtype)],
        input_output_aliases={2: 0, 3: 1},
        compiler_params=_cp("arbitrary"),
        name="context_cache_store",
    )(k, v, kc, vc)


def _softmax_pv(score_blocks, value_blocks):
    m = score_blocks[0].max(axis=-1, keepdims=True)
    for s in score_blocks[1:]:
        m = jnp.maximum(m, s.max(axis=-1, keepdims=True))
    den = None
    out = None
    for s, v in zip(score_blocks, value_blocks):
        p = jnp.exp(s - m)
        ps = p.sum(axis=-1, keepdims=True)
        pv = jnp.dot(p.astype(BF16), v, preferred_element_type=F32)
        den = ps if den is None else den + ps
        out = pv if out is None else out + pv
    return out * (1.0 / den)


def _attn_ctx_kernel(q_ref, k_ref, v_ref, mix_ref, o_ref, *, nkv, qpk, hd):
    s_len = q_ref.shape[0]
    for kv in range(nkv):
        kb = k_ref[:, kv * hd:(kv + 1) * hd].astype(BF16)
        vb = v_ref[:, kv * hd:(kv + 1) * hd].astype(BF16)
        q3 = jnp.concatenate(
            [q_ref[:, (kv * qpk + g) * hd:(kv * qpk + g + 1) * hd] for g in range(qpk)], axis=0)
        sc = lax.dot_general(q3, kb, _NT, preferred_element_type=F32)
        o = _softmax_pv([sc], [vb])
        for g in range(qpk):
            o_ref[:, (kv * qpk + g) * hd:(kv * qpk + g + 1) * hd] = (
                o[g * s_len:(g + 1) * s_len].astype(o_ref.dtype))


def _attn_lat_kernel(q_ref, k_ref, v_ref, ck_ref, cv_ref, mix_ref, o_ref, *, qpk, hd):
    kb = k_ref[...].astype(BF16)
    ckb = ck_ref[...].astype(BF16)
    vb = v_ref[...].astype(BF16)
    cvb = cv_ref[...].astype(BF16)
    for g in range(qpk):
        qg = q_ref[:, g * hd:(g + 1) * hd]
        s1 = lax.dot_general(qg, kb, _NT, preferred_element_type=F32)
        s2 = lax.dot_general(qg, ckb, _NT, preferred_element_type=F32)
        o_ref[:, g * hd:(g + 1) * hd] = _softmax_pv([s1, s2], [vb, cvb]).astype(o_ref.dtype)


def _attention(q, k, v, cache_k4, cache_v4, mix, layer, dims):
    t, aw = q.shape
    hd, nkv, qpk = dims["hd"], dims["nkv"], dims["qpk"]
    seq, dec_seq, t_ctx, past = dims["seq"], dims["dec_seq"], dims["t_ctx"], dims["past"]
    n_ctx = t_ctx // seq
    kvw = nkv * hd
    mix = pl.pallas_call(
        functools.partial(_attn_ctx_kernel, nkv=nkv, qpk=qpk, hd=hd),
        grid=(n_ctx,),
        in_specs=[
            pl.BlockSpec((seq, aw), lambda b: (b, 0)),
            pl.BlockSpec((seq, kvw), lambda b: (b, 0)),
            pl.BlockSpec((seq, kvw), lambda b: (b, 0)),
            pl.BlockSpec(memory_space=pl.ANY),
        ],
        out_specs=pl.BlockSpec((seq, aw), lambda b: (b, 0)),
        out_shape=jax.ShapeDtypeStruct(mix.shape, mix.dtype),
        input_output_aliases={3: 0},
        compiler_params=_cp("arbitrary"),
        name="attention_context",
    )(q, k, v, mix)

    assert t_ctx % dec_seq == 0
    n_lat = (t - t_ctx) // dec_seq
    tq = _tile(dec_seq, 512)
    nqb = dec_seq // tq
    q_off = t_ctx // tq
    k_off = t_ctx // dec_seq
    qw = qpk * hd
    return pl.pallas_call(
        functools.partial(_attn_lat_kernel, qpk=qpk, hd=hd),
        grid=(n_lat, nkv, nqb),
        in_specs=[
            pl.BlockSpec((tq, qw), lambda b, kv, qi: (q_off + b * nqb + qi, kv)),
            pl.BlockSpec((dec_seq, hd), lambda b, kv, qi: (k_off + b, kv)),
            pl.BlockSpec((dec_seq, hd), lambda b, kv, qi: (k_off + b, kv)),
            pl.BlockSpec((None, None, past, hd), lambda b, kv, qi: (b, layer, 0, kv)),
            pl.BlockSpec((None, None, past, hd), lambda b, kv, qi: (b, layer, 0, kv)),
            pl.BlockSpec(memory_space=pl.ANY),
        ],
        out_specs=pl.BlockSpec((tq, qw), lambda b, kv, qi: (q_off + b * nqb + qi, kv)),
        out_shape=jax.ShapeDtypeStruct(mix.shape, mix.dtype),
        input_output_aliases={5: 0},
        compiler_params=_cp("arbitrary", "arbitrary", "arbitrary"),
        name="attention_latent",
    )(q, k, v, cache_k4, cache_v4, mix)


def _fourier_kernel(u_ref, dpos_ref, dch_ref, wf_ref, mix_ref, o_ref, tt_ref, *, ng, gd, norm):
    s_len = u_ref.shape[0]
    for g in range(ng):
        sl = slice(g * gd, (g + 1) * gd)
        t = jnp.dot(u_ref[:, sl], dch_ref[...], preferred_element_type=F32)
        tt_ref[0:s_len, sl] = t[:, :gd].astype(BF16)
        tt_ref[s_len:2 * s_len, sl] = t[:, gd:].astype(BF16)
    f = (jnp.dot(dpos_ref[...], tt_ref[...], preferred_element_type=F32) * norm).astype(BF16)
    for g in range(ng):
        sl = slice(g * gd, (g + 1) * gd)
        o_ref[:, sl] = jnp.dot(f[:, sl], wf_ref[g].astype(BF16),
                               preferred_element_type=F32).astype(o_ref.dtype)


def _dft_tables(n):
    jk = (np.arange(n, dtype=np.int64)[:, None] * np.arange(n, dtype=np.int64)[None, :]) % n
    ang = jk.astype(np.float64) * (2.0 * np.pi / n)
    return np.cos(ang), np.sin(ang)


def _fourier(u, w_fourier, mix, layer, s_len, row_off, n_seq, tables):
    ng, gd = w_fourier.shape[1], w_fourier.shape[2]
    fw = ng * gd
    assert (mix.shape[1] - fw) % fw == 0 and row_off % s_len == 0
    col = (mix.shape[1] - fw) // fw
    dpos, dch = tables
    off = row_off // s_len
    return pl.pallas_call(
        functools.partial(_fourier_kernel, ng=ng, gd=gd, norm=float(s_len * gd) ** -0.5),
        grid=(n_seq,),
        in_specs=[
            pl.BlockSpec((s_len, fw), lambda b: (off + b, 0)),
            pl.BlockSpec((s_len, 2 * s_len), lambda b: (0, 0)),
            pl.BlockSpec((gd, 2 * gd), lambda b: (0, 0)),
            pl.BlockSpec((None, ng, gd, gd), lambda b: (layer, 0, 0, 0)),
            pl.BlockSpec(memory_space=pl.ANY),
        ],
        out_specs=pl.BlockSpec((s_len, fw), lambda b: (off + b, col)),
        out_shape=jax.ShapeDtypeStruct(mix.shape, mix.dtype),
        input_output_aliases={4: 0},
        scratch_shapes=[pltpu.VMEM((2 * s_len, fw), BF16)],
        compiler_params=_cp("arbitrary"),
        name="fourier_mix",
    )(u, dpos, dch, w_fourier, mix)


def _wout_kernel(m_ref, w_ref, x_ref, g_ref, o_ref):
    acc = jnp.dot(m_ref[...], w_ref[...].astype(BF16), preferred_element_type=F32)
    o_ref[...] = x_ref[...] + g_ref[...] * acc


def _out_project(mix, w_out, x, mods, layer, dims):
    t, d = x.shape
    mw = mix.shape[1]
    rows = _Rows(dims["t_ctx"], dims["dec_seq"], _tile(min(dims["t_ctx"], dims["dec_seq"]), 1024))
    tm = rows.tm
    tn = _tile(d, 512)
    return pl.pallas_call(
        _wout_kernel,
        grid=(t // tm, d // tn),
        in_specs=[
            pl.BlockSpec((tm, mw), lambda i, j: (i, 0)),
            pl.BlockSpec((None, mw, tn), lambda i, j: (layer, 0, j)),
            pl.BlockSpec((tm, tn), lambda i, j: (i, j)),
            _mod_spec(rows, layer, 2, tn, col=lambda j: j),
        ],
        out_specs=pl.BlockSpec((tm, tn), lambda i, j: (i, j)),
        out_shape=jax.ShapeDtypeStruct((t, d), F32),
        compiler_params=_cp("arbitrary", "arbitrary"),
        name="out_projection",
    )(mix, w_out, x, mods)


def _top2_sum(a, b, c, d):
    hi1, lo1 = jnp.maximum(a, b), jnp.minimum(a, b)
    hi2, lo2 = jnp.maximum(c, d), jnp.minimum(c, d)
    return jnp.maximum(hi1, hi2) + jnp.maximum(jnp.minimum(hi1, hi2), jnp.maximum(lo1, lo2))


def _norm_router_kernel(x_ref, g_ref, sc_ref, sh_ref, wr_ref, rb_ref, hp_ref, idx_ref, wt_ref,
                        *, n_exp, n_grp):
    hf = _norm_mod(x_ref[...], g_ref[...], sc_ref[...], sh_ref[...])
    half = hf.shape[1] // 2
    hp_ref[...] = _pack_pair(hf[:, :half], hf[:, half:])

    hi = hf.astype(BF16)
    lo = (hf - hi.astype(F32)).astype(BF16)
    w = wr_ref[...]
    whi = w.astype(BF16)
    wlo = (w - whi.astype(F32)).astype(BF16)
    lg = (lax.dot_general(whi, hi, _NT, preferred_element_type=F32)
          + lax.dot_general(whi, lo, _NT, preferred_element_type=F32)
          + lax.dot_general(wlo, hi, _NT, preferred_element_type=F32))
    ex = jnp.exp(lg - lg.max(axis=0, keepdims=True))
    aff = ex * (1.0 / ex.sum(axis=0, keepdims=True))
    sel = aff + rb_ref[...]

    per = n_exp // n_grp
    row = lambda a, e: a[e:e + 1, :]
    assert per == 4, "group score uses a 4-element top-2 network"
    best_s = _top2_sum(*[row(sel, e) for e in range(per)])
    best_g = jnp.zeros_like(best_s, dtype=jnp.int32)
    for g in range(1, n_grp):
        s = _top2_sum(*[row(sel, g * per + e) for e in range(per)])
        upd = s > best_s
        best_g = jnp.where(upd, g, best_g)
        best_s = jnp.where(upd, s, best_s)

    neg = jnp.float32(-jnp.inf)
    picks = []
    for _ in range(2):
        bv = jnp.full_like(best_s, neg)
        bi = jnp.zeros_like(best_g)
        ba = jnp.zeros_like(best_s)
        for e in range(n_exp):
            ok = best_g == (e // per)
            for (pi, _) in picks:
                ok = jnp.logical_and(ok, pi != e)
            val = jnp.where(ok, row(sel, e), neg)
            upd = val > bv
            bv = jnp.where(upd, val, bv)
            bi = jnp.where(upd, e, bi)
            ba = jnp.where(upd, row(aff, e), ba)
        picks.append((bi, ba))
    (i0, a0), (i1, a1) = picks
    inv = 1.0 / (a0 + a1)
    idx_ref[0:1, :] = i0
    idx_ref[1:2, :] = i1
    wt_ref[0:1, :] = a0 * inv
    wt_ref[1:2, :] = a1 * inv


def _norm_router(x, gain, mods, w_router_t, router_bias, layer, dims):
    t, d = x.shape
    n_exp = w_router_t.shape[0]
    rows = _Rows(dims["t_ctx"], dims["dec_seq"], _tile(min(dims["t_ctx"], dims["dec_seq"]), 256))
    tm = rows.tm
    return pl.pallas_call(
        functools.partial(_norm_router_kernel, n_exp=n_exp, n_grp=N_EXPERT_GROUPS),
        grid=(t // tm,),
        in_specs=[
            pl.BlockSpec((tm, d), lambda i: (i, 0)),
            pl.BlockSpec((None, 1, d), lambda i: (layer, 0, 0)),
            _mod_spec(rows, layer, 4, d),
            _mod_spec(rows, layer, 3, d),
            pl.BlockSpec((n_exp, d), lambda i: (0, 0)),
            pl.BlockSpec((n_exp, 1), lambda i: (0, 0)),
        ],
        out_specs=[
            pl.BlockSpec((tm, d // 2), lambda i: (i, 0)),
            pl.BlockSpec((2, tm), lambda i: (0, i)),
            pl.BlockSpec((2, tm), lambda i: (0, i)),
        ],
        out_shape=[
            jax.ShapeDtypeStruct((t, d // 2), U32),
            jax.ShapeDtypeStruct((2, t), jnp.int32),
            jax.ShapeDtypeStruct((2, t), F32),
        ],
        compiler_params=_cp("arbitrary"),
        name="norm_router",
    )(x, gain, mods, mods, w_router_t, router_bias.reshape(n_exp, 1))


def _dispatch_plan(idx, n_exp, item_rows, sub_rows, n_items):
    t = idx.shape[1]
    e_a = idx.reshape(-1)
    onehot = (e_a[:, None] == jnp.arange(n_exp, dtype=jnp.int32)[None, :]).astype(jnp.int32)
    csum = jnp.cumsum(onehot, axis=0)
    counts = csum[-1]
    rank = jnp.take_along_axis(csum, e_a[:, None], axis=1)[:, 0] - 1
    items_e = (counts + item_rows - 1) // item_rows
    item_end = jnp.cumsum(items_e)
    item_start = item_end - items_e
    pos = item_start[e_a] * item_rows + rank
    tok = jnp.tile(jnp.arange(t, dtype=jnp.int32), 2)
    slot_tok = jnp.zeros((n_items * item_rows,), jnp.int32).at[pos].set(tok)
    w = jnp.arange(n_items, dtype=jnp.int32)
    item_exp = jnp.minimum(jnp.searchsorted(item_end, w, side="right"), n_exp - 1).astype(jnp.int32)
    rows = jnp.clip(counts[item_exp] - (w - item_start[item_exp]) * item_rows, 0, item_rows)
    n_sub = jnp.where(w < item_end[-1], (rows + sub_rows - 1) // sub_rows, 0).astype(jnp.int32)
    n_valid = item_end[-1].astype(jnp.int32).reshape(1)
    return slot_tok, pos.reshape(2, t), item_exp, n_sub, n_valid


def _row_copy(src_hbm, dst, sem, src_row, dst_row, n=1):
    return pltpu.make_async_copy(src_hbm.at[pl.ds(src_row, n)], dst.at[pl.ds(dst_row, n)], sem)


GATHER_UNROLL = 8


def _start_row_gather(src_hbm, dst, sem, idx_ref, n_rows):
    def issue(g, carry):
        base = g * GATHER_UNROLL
        for u in range(GATHER_UNROLL):
            _row_copy(src_hbm, dst, sem, idx_ref[0, 0, base + u], base + u).start()
        return carry

    lax.fori_loop(0, n_rows // GATHER_UNROLL, issue, 0)


def _zero_tail(o_ref, first_sub, n_sub_total, sub):
    def body(s, carry):
        o_ref[pl.ds(pl.multiple_of(s * sub, sub), sub), :] = jnp.zeros((sub, o_ref.shape[1]), o_ref.dtype)
        return carry

    lax.fori_loop(first_sub, n_sub_total, body, 0)


def _for_live_rows(n_sub, sub, group, fn):
    big = group * sub
    n_big = n_sub // group

    def big_body(s, carry):
        fn(pl.ds(pl.multiple_of(s * big, big), big))
        return carry

    lax.fori_loop(0, n_big, big_body, 0)

    def small_body(s, carry):
        fn(pl.ds(pl.multiple_of(s * sub, sub), sub))
        return carry

    lax.fori_loop(n_big * group, n_sub, small_body, 0)


def _ffn_up_kernel(te_ref, ns_ref, nv_ref, tok_ref, tok_next_ref, hp_ref, w1_ref, w3_ref, o_ref,
                   xbuf, xl_ref, xr_ref, wb1_ref, wb3_ref, sem, *, sub, group):
    w = pl.program_id(0)
    c = pl.program_id(1)
    nv = nv_ref[0]
    n_sub_total = o_ref.shape[0] // sub
    half = xl_ref.shape[1]

    def start_gather(toks, n_sub):
        _start_row_gather(hp_ref, xbuf, sem, toks, n_sub * sub)

    @pl.when(jnp.logical_and(w == 0, c == 0))
    def _():
        start_gather(tok_ref, ns_ref[0])

    @pl.when(jnp.logical_and(c == 0, w < nv))
    def _():
        def land(s, carry):
            _row_copy(hp_ref, xbuf, sem, 0, 0, sub).wait()
            return carry

        lax.fori_loop(0, ns_ref[w], land, 0)

        def unpack(s, carry):
            rows = pl.ds(pl.multiple_of(s * sub, sub), sub)
            lo, hi = _unpack_pair(xbuf[rows, :])
            xl_ref[rows, :] = lo.astype(BF16)
            xr_ref[rows, :] = hi.astype(BF16)
            return carry

        lax.fori_loop(0, ns_ref[w], unpack, 0)

        @pl.when(w + 1 < nv)
        def _():
            start_gather(tok_next_ref, ns_ref[jnp.minimum(w + 1, ns_ref.shape[0] - 1)])

    @pl.when(w < nv)
    def _():
        wb1_ref[...] = w1_ref[...].astype(BF16)
        wb3_ref[...] = w3_ref[...].astype(BF16)

        def body(rows):
            xl = xl_ref[rows, :]
            xr = xr_ref[rows, :]
            a = (jnp.dot(xl, wb1_ref[:half, :], preferred_element_type=F32)
                 + jnp.dot(xr, wb1_ref[half:, :], preferred_element_type=F32))
            b = (jnp.dot(xl, wb3_ref[:half, :], preferred_element_type=F32)
                 + jnp.dot(xr, wb3_ref[half:, :], preferred_element_type=F32))
            o_ref[rows, :] = (a * jax.nn.sigmoid(a) * b).astype(o_ref.dtype)

        _for_live_rows(ns_ref[w], sub, group, body)
        _zero_tail(o_ref, ns_ref[w], n_sub_total, sub)

    @pl.when(w >= nv)
    def _():
        o_ref[...] = jnp.zeros_like(o_ref)


def _ffn_down_kernel(te_ref, ns_ref, nv_ref, h_ref, wl_ref, wr_ref, o_ref, wbl_ref, wbr_ref,
                     *, sub, group):
    w = pl.program_id(0)
    n_sub_total = o_ref.shape[0] // sub

    @pl.when(w < nv_ref[0])
    def _():
        wbl_ref[...] = wl_ref[...].astype(BF16)
        wbr_ref[...] = wr_ref[...].astype(BF16)

        def body(rows):
            h = h_ref[rows, :]
            yl = jnp.dot(h, wbl_ref[...], preferred_element_type=F32)
            yr = jnp.dot(h, wbr_ref[...], preferred_element_type=F32)
            o_ref[rows, :] = _pack_pair(yl, yr)

        _for_live_rows(ns_ref[w], sub, group, body)
        _zero_tail(o_ref, ns_ref[w], n_sub_total, sub)

    @pl.when(w >= nv_ref[0])
    def _():
        o_ref[...] = jnp.zeros_like(o_ref)


def _expert_ffn(hp, slot_tok, item_exp, n_sub, n_valid, w1, w3, w2, layer, item_rows, sub_rows):
    t, dh = hp.shape
    d = 2 * dh
    f = w1.shape[3]
    n_items = item_exp.shape[0]
    r_total = n_items * item_rows
    tf = _tile(f, 256)
    nf = f // tf
    toks = slot_tok.reshape(n_items, 1, item_rows)

    def live(w, nv):
        return jnp.minimum(w, nv[0] - 1)

    def chunk(w, c, nv, n_chunks):
        return jnp.where(w < nv[0], c, n_chunks - 1)

    hid = pl.pallas_call(
        functools.partial(_ffn_up_kernel, sub=sub_rows, group=MOE_GROUP_SUBS),
        grid_spec=pltpu.PrefetchScalarGridSpec(
            num_scalar_prefetch=3,
            grid=(n_items, nf),
            in_specs=[
                pl.BlockSpec((1, 1, item_rows), lambda w, c, te, ns, nv: (w, 0, 0),
                             memory_space=pltpu.SMEM),
                pl.BlockSpec((1, 1, item_rows),
                             lambda w, c, te, ns, nv: (jnp.minimum(w + 1, n_items - 1), 0, 0),
                             memory_space=pltpu.SMEM),
                pl.BlockSpec(memory_space=pl.ANY),
                pl.BlockSpec((None, None, d, tf),
                             lambda w, c, te, ns, nv: (layer, te[live(w, nv)], 0, chunk(w, c, nv, nf))),
                pl.BlockSpec((None, None, d, tf),
                             lambda w, c, te, ns, nv: (layer, te[live(w, nv)], 0, chunk(w, c, nv, nf))),
            ],
            out_specs=pl.BlockSpec((item_rows, tf), lambda w, c, te, ns, nv: (w, c)),
            scratch_shapes=[
                pltpu.VMEM((item_rows, dh), U32),
                pltpu.VMEM((item_rows, dh), BF16),
                pltpu.VMEM((item_rows, dh), BF16),
                pltpu.VMEM((d, tf), BF16),
                pltpu.VMEM((d, tf), BF16),
                pltpu.SemaphoreType.DMA,
            ],
        ),
        out_shape=jax.ShapeDtypeStruct((r_total, f), BF16),
        compiler_params=_cp("arbitrary", "arbitrary"),
        name="expert_ffn_up",
    )(item_exp, n_sub, n_valid, toks, toks, hp, w1, w3)

    tn = _tile(dh, 512)
    nn = dh // tn
    return pl.pallas_call(
        functools.partial(_ffn_down_kernel, sub=sub_rows, group=MOE_GROUP_SUBS),
        grid_spec=pltpu.PrefetchScalarGridSpec(
            num_scalar_prefetch=3,
            grid=(n_items, nn),
            in_specs=[
                pl.BlockSpec((item_rows, f), lambda w, c, te, ns, nv: (live(w, nv), 0)),
                pl.BlockSpec((None, None, f, tn),
                             lambda w, c, te, ns, nv: (layer, te[live(w, nv)], 0, chunk(w, c, nv, nn))),
                pl.BlockSpec((None, None, f, tn),
                             lambda w, c, te, ns, nv: (layer, te[live(w, nv)], 0,
                                                       nn + chunk(w, c, nv, nn))),
            ],
            out_specs=pl.BlockSpec((item_rows, tn), lambda w, c, te, ns, nv: (w, c)),
            scratch_shapes=[pltpu.VMEM((f, tn), BF16), pltpu.VMEM((f, tn), BF16)],
        ),
        out_shape=jax.ShapeDtypeStruct((r_total, dh), U32),
        compiler_params=_cp("arbitrary", "arbitrary"),
        name="expert_ffn_down",
    )(item_exp, n_sub, n_valid, hid, w2, w2)


def _combine_kernel(pos_ref, pos_next_ref, y_ref, x_ref, g_ref, wt_ref, *rest, tm, with_next_norm):
    if with_next_norm:
        gn_ref, scn_ref, shn_ref, o_ref, h_ref, ybuf, sem = rest
    else:
        o_ref, ybuf, sem = rest
    i = pl.program_id(0)
    slot = i % 2

    def start_gather(poss, dst_slot):
        _start_row_gather(y_ref, ybuf.at[dst_slot], sem.at[dst_slot], poss, 2 * tm)

    @pl.when(i == 0)
    def _():
        start_gather(pos_ref, 0)

    @pl.when(i + 1 < pl.num_programs(0))
    def _():
        start_gather(pos_next_ref, 1 - slot)

    _row_copy(y_ref, ybuf.at[slot], sem.at[slot], 0, 0, 2 * tm).wait()

    half = ybuf.shape[2]
    w0 = wt_ref[:, 0:1]
    w1 = wt_ref[:, 1:2]
    l0, r0 = _unpack_pair(ybuf[slot, 0:tm, :])
    l1, r1 = _unpack_pair(ybuf[slot, tm:2 * tm, :])
    ol = x_ref[:, :half] + g_ref[:, :half] * (w0 * l0 + w1 * l1)
    orr = x_ref[:, half:] + g_ref[:, half:] * (w0 * r0 + w1 * r1)
    o_ref[:, :half] = ol
    o_ref[:, half:] = orr
    if with_next_norm:
        ms = (jnp.sum(ol * ol, axis=-1, keepdims=True)
              + jnp.sum(orr * orr, axis=-1, keepdims=True)) * (1.0 / (2 * half))
        inv = lax.rsqrt(ms + EPS)
        for sl, part in ((slice(0, half), ol), (slice(half, 2 * half), orr)):
            yn = part * inv * gn_ref[:, sl]
            h_ref[:, sl] = (yn * (1.0 + scn_ref[:, sl]) + shn_ref[:, sl]).astype(h_ref.dtype)


def _combine(y, pos, wts, x, mods, layer, dims, next_gain=None):
    t, d = x.shape
    rows = _Rows(dims["t_ctx"], dims["dec_seq"], _tile(min(dims["t_ctx"], dims["dec_seq"]), 256))
    tm = rows.tm
    nt = t // tm
    with_next = next_gain is not None
    pos_tiles = pos.reshape(2, nt, tm).transpose(1, 0, 2).reshape(nt, 1, 2 * tm)
    row_spec = pl.BlockSpec((tm, d), lambda i: (i, 0))
    in_specs = [
        pl.BlockSpec((1, 1, 2 * tm), lambda i: (i, 0, 0), memory_space=pltpu.SMEM),
        pl.BlockSpec((1, 1, 2 * tm), lambda i: (jnp.minimum(i + 1, nt - 1), 0, 0),
                     memory_space=pltpu.SMEM),
        pl.BlockSpec(memory_space=pl.ANY),
        row_spec,
        _mod_spec(rows, layer, 5, d),
        pl.BlockSpec((tm, 2), lambda i: (i, 0)),
    ]
    args = [pos_tiles, pos_tiles, y, x, mods, wts.T]
    out_specs, out_shape = row_spec, jax.ShapeDtypeStruct((t, d), F32)
    if with_next:
        in_specs += [pl.BlockSpec((None, 1, d), lambda i: (layer + 1, 0, 0)),
                     _mod_spec(rows, layer + 1, 1, d), _mod_spec(rows, layer + 1, 0, d)]
        args += [next_gain, mods, mods]
        out_specs = [row_spec, row_spec]
        out_shape = [out_shape, jax.ShapeDtypeStruct((t, d), BF16)]
    return pl.pallas_call(
        functools.partial(_combine_kernel, tm=tm, with_next_norm=with_next),
        grid=(nt,),
        in_specs=in_specs,
        out_specs=out_specs,
        out_shape=out_shape,
        scratch_shapes=[pltpu.VMEM((2, 2 * tm, d // 2), U32), pltpu.SemaphoreType.DMA((2,))],
        compiler_params=_cp("arbitrary"),
        name="moe_combine",
    )(*args)


MOE_ITEM_ROWS = 1536
MOE_SUB_ROWS = 128
MOE_GROUP_SUBS = 4


def kernel(x_prompt, x_sample, cache_k, cache_v, c, c_ctx, w_ada, b_ada, norm_mix, norm_ffn, w_in, q_norm, k_norm, w_fourier, w_out, w_router, router_bias, w1, w3, w2):
    batch, seq, d = x_prompt.shape
    dec_batch, dec_seq, _ = x_sample.shape
    depth = w_in.shape[0]
    past, nkv, hd = cache_k.shape[2], cache_k.shape[3], cache_k.shape[4]
    ng, gd = w_fourier.shape[1], w_fourier.shape[2]
    fw = ng * gd
    kvw = nkv * hd
    aw = w_out.shape[1] - fw
    n_exp = w_router.shape[1]
    t_ctx = batch * seq
    t_lat = dec_batch * dec_seq
    t = t_ctx + t_lat
    assert hd == LANE and dec_batch + 1 <= COND_ROWS and w_in.shape[2] == aw + 2 * kvw + fw
    dims = dict(hd=hd, aw=aw, kvw=kvw, fw=fw, nkv=nkv, qpk=aw // kvw, seq=seq, dec_seq=dec_seq,
                t_ctx=t_ctx, past=past)

    cond = jnp.zeros((COND_ROWS, d), F32).at[0].set(c_ctx).at[1:1 + dec_batch].set(c)
    mods = _modulation(cond, w_ada, b_ada).reshape(depth, COND_ROWS, N_MOD, 1, d)

    rope = _rope_tables(dec_seq, hd)
    cs_c, sn_c = _dft_tables(seq)
    cs_l, sn_l = _dft_tables(dec_seq)
    cs_g, sn_g = _dft_tables(gd)
    bf_const = lambda a: jnp.asarray(a.astype(np.float32).astype(BF16))
    dch = bf_const(np.concatenate([cs_g, sn_g], axis=1))
    dft_ctx = (bf_const(np.concatenate([cs_c, -sn_c], axis=1)), dch)
    dft_lat = (bf_const(np.concatenate([cs_l, -sn_l], axis=1)), dch)

    cache_k4 = cache_k.reshape(dec_batch, depth, past, kvw)
    cache_v4 = cache_v.reshape(dec_batch, depth, past, kvw)
    w_router_t = w_router.T
    n_moe_items = (2 * t) // MOE_ITEM_ROWS + n_exp
    mix = jnp.zeros((t, aw + fw), BF16)
    norm_mix3 = norm_mix.reshape(depth, 1, d)
    norm_ffn3 = norm_ffn.reshape(depth, 1, d)
    q_norm3 = q_norm.reshape(depth, 1, hd)
    k_norm3 = k_norm.reshape(depth, 1, hd)

    x = jnp.concatenate([x_prompt.reshape(t_ctx, d), x_sample.reshape(t_lat, d)], axis=0)
    new_k = jnp.zeros((batch, depth, seq, kvw), F32)
    new_v = jnp.zeros((batch, depth, seq, kvw), F32)
    h = _norm_modulate(x, norm_mix3, mods, 0, t_ctx, dec_seq)
    for l in range(depth):
        q, k, v, u = _project(h, w_in, q_norm3, k_norm3, rope, l, dims)
        new_k, new_v = _store_context_cache(k, v, new_k, new_v, l, dims)
        mix = _attention(q, k, v, cache_k4, cache_v4, mix, l, dims)
        mix = _fourier(u, w_fourier, mix, l, seq, 0, batch, dft_ctx)
        mix = _fourier(u, w_fourier, mix, l, dec_seq, t_ctx, dec_batch, dft_lat)
        x = _out_project(mix, w_out, x, mods, l, dims)

        hp, idx, wts = _norm_router(x, norm_ffn3, mods, w_router_t, router_bias, l, dims)
        slot_tok, pos, item_exp, n_sub, n_valid = _dispatch_plan(
            idx, n_exp, MOE_ITEM_ROWS, MOE_SUB_ROWS, n_moe_items)
        y = _expert_ffn(hp, slot_tok, item_exp, n_sub, n_valid, w1, w3, w2, l,
                        MOE_ITEM_ROWS, MOE_SUB_ROWS)
        if l + 1 < depth:
            x, h = _combine(y, pos, wts, x, mods, l, dims, next_gain=norm_mix3)
        else:
            x = _combine(y, pos, wts, x, mods, l, dims)

    y_prompt = x[:t_ctx].reshape(batch, seq, d)
    y_sample = x[t_ctx:].reshape(dec_batch, dec_seq, d)
    cache_shape = (batch, depth, seq, nkv, hd)
    return (y_prompt, y_sample, new_k.reshape(cache_shape), new_v.reshape(cache_shape))
```

```python
import functools
import math

import jax
import jax.numpy as jnp
import numpy as np
from jax import lax
from jax.experimental import pallas as pl
from jax.experimental.pallas import tpu as pltpu

GRID_W = 64
N_EXPERT_GROUPS = 4
ROPE_THETA = 10000.0
EPS = 1e-6
N_MOD = 6
COND_ROWS = 8

VMEM_LIMIT_BYTES = 56 * 1024 * 1024
LANE = 128

BF16 = jnp.bfloat16
F32 = jnp.float32
U32 = jnp.uint32

_NT = (((1,), (1,)), ((), ()))


def _cp(*sem):
    return pltpu.CompilerParams(dimension_semantics=sem, vmem_limit_bytes=VMEM_LIMIT_BYTES)


def _tile(n, pref):
    t = min(n, pref)
    while n % t:
        t -= 1
    return t


def _pack_pair(a, b):
    ua = lax.bitcast_convert_type(a.astype(BF16).astype(F32), U32)
    ub = lax.bitcast_convert_type(b.astype(BF16).astype(F32), U32)
    return (ua >> 16) | (ub & jnp.uint32(0xFFFF0000))


def _unpack_pair(w):
    lo = lax.bitcast_convert_type(w << 16, F32)
    hi = lax.bitcast_convert_type(w & jnp.uint32(0xFFFF0000), F32)
    return lo, hi


def _mod_kernel(c_ref, w_ref, b_ref, o_ref):
    c = c_ref[...]
    s = (c * jax.nn.sigmoid(c)).astype(BF16)
    o_ref[...] = jnp.dot(s, w_ref[...].astype(BF16), preferred_element_type=F32) + b_ref[...]


def _modulation(cond, w_ada, b_ada):
    depth, d, n = w_ada.shape
    tn = _tile(n, 512)
    return pl.pallas_call(
        _mod_kernel,
        grid=(depth, n // tn),
        in_specs=[
            pl.BlockSpec((COND_ROWS, d), lambda l, j: (0, 0)),
            pl.BlockSpec((None, d, tn), lambda l, j: (l, 0, j)),
            pl.BlockSpec((None, 1, tn), lambda l, j: (l, 0, j)),
        ],
        out_specs=pl.BlockSpec((None, COND_ROWS, tn), lambda l, j: (l, 0, j)),
        out_shape=jax.ShapeDtypeStruct((depth, COND_ROWS, n), F32),
        compiler_params=_cp("arbitrary", "arbitrary"),
        name="modulation",
    )(cond, w_ada, b_ada.reshape(depth, 1, n))


class _Rows:
    def __init__(self, t_ctx, dec_seq, tm):
        assert t_ctx % tm == 0 and dec_seq % tm == 0
        self.tm = tm
        self.n_ctx_tiles = t_ctx // tm
        self.tiles_per_seq = dec_seq // tm

    def cond(self, i):
        lat = 1 + (i - self.n_ctx_tiles) // self.tiles_per_seq
        return jnp.where(i < self.n_ctx_tiles, 0, lat)

    def seq_tile(self, i):
        return jnp.maximum(i - self.n_ctx_tiles, 0) % self.tiles_per_seq


def _mod_spec(rows, layer, which, width, col=None):
    if col is None:
        return pl.BlockSpec((None, None, None, 1, width),
                            lambda i, *_: (layer, rows.cond(i), which, 0, 0))
    return pl.BlockSpec((None, None, None, 1, width),
                        lambda i, j, *_: (layer, rows.cond(i), which, 0, col(j)))


def _norm_mod(x, g, sc, sh):
    ms = jnp.mean(x * x, axis=-1, keepdims=True)
    y = x * lax.rsqrt(ms + EPS) * g
    return y * (1.0 + sc) + sh


def _stack_norm_mod_kernel(xc_ref, xl_ref, g_ref, sc_ref, sh_ref, x_ref, h_ref, *, n_ctx_tiles):
    def emit(src_ref):
        x = src_ref[...]
        x_ref[...] = x
        h_ref[...] = _norm_mod(x, g_ref[...], sc_ref[...], sh_ref[...]).astype(h_ref.dtype)

    @pl.when(pl.program_id(0) < n_ctx_tiles)
    def _():
        emit(xc_ref)

    @pl.when(pl.program_id(0) >= n_ctx_tiles)
    def _():
        emit(xl_ref)


def _stack_norm_modulate(x_ctx, x_lat, gain, mods, layer, dec_seq):
    t_ctx, d = x_ctx.shape
    t = t_ctx + x_lat.shape[0]
    rows = _Rows(t_ctx, dec_seq, _tile(min(t_ctx, dec_seq), 256))
    tm = rows.tm
    nct = rows.n_ctx_tiles
    row_spec = pl.BlockSpec((tm, d), lambda i: (i, 0))
    return pl.pallas_call(
        functools.partial(_stack_norm_mod_kernel, n_ctx_tiles=nct),
        grid=(t // tm,),
        in_specs=[
            pl.BlockSpec((tm, d), lambda i: (jnp.minimum(i, nct - 1), 0)),
            pl.BlockSpec((tm, d), lambda i: (jnp.maximum(i - nct, 0), 0)),
            pl.BlockSpec((None, 1, d), lambda i: (layer, 0, 0)),
            _mod_spec(rows, layer, 1, d),
            _mod_spec(rows, layer, 0, d),
        ],
        out_specs=[row_spec, row_spec],
        out_shape=[jax.ShapeDtypeStruct((t, d), F32), jax.ShapeDtypeStruct((t, d), BF16)],
        compiler_params=_cp("arbitrary"),
        name="stack_norm_modulate",
    )(x_ctx, x_lat, gain, mods, mods)


def _proj_kernel(h_ref, w_ref, qg_ref, kg_ref, cos_ref, se_ref, so_ref,
                 q_ref, k_ref, v_ref, u_ref, *, nq, nk, nv, n_ctx_tiles, hd, q_scale):
    i = pl.program_id(0)
    j = pl.program_id(1)
    acc = jnp.dot(h_ref[...], w_ref[...].astype(BF16), preferred_element_type=F32)
    heads = acc.shape[1] // hd

    def head_norm(gain_ref, out_ref, mult, rope):
        for hh in range(heads):
            sl = slice(hh * hd, (hh + 1) * hd)
            blk = acc[:, sl]
            ms = jnp.mean(blk * blk, axis=-1, keepdims=True)
            y = blk * lax.rsqrt(ms + EPS) * gain_ref[...]
            if rope:
                y = (y * cos_ref[...] + pltpu.roll(y, hd - 1, 1) * se_ref[...]
                     + pltpu.roll(y, 1, 1) * so_ref[...])
            if mult != 1.0:
                y = y * mult
            out_ref[:, sl] = y.astype(out_ref.dtype)

    def qk_path(gain_ref, out_ref, mult):
        @pl.when(i < n_ctx_tiles)
        def _():
            head_norm(gain_ref, out_ref, mult, False)

        @pl.when(i >= n_ctx_tiles)
        def _():
            head_norm(gain_ref, out_ref, mult, True)

    @pl.when(j < nq)
    def _():
        qk_path(qg_ref, q_ref, q_scale)

    @pl.when(jnp.logical_and(j >= nq, j < nq + nk))
    def _():
        qk_path(kg_ref, k_ref, 1.0)

    @pl.when(jnp.logical_and(j >= nq + nk, j < nq + nk + nv))
    def _():
        v_ref[...] = acc

    @pl.when(j >= nq + nk + nv)
    def _():
        u_ref[...] = acc.astype(u_ref.dtype)


def _rope_tables(n_tok, hd):
    axis_dim = hd // 2
    rows = n_tok // GRID_W
    row_idx = np.repeat(np.arange(rows, dtype=np.float64), GRID_W)
    col_idx = np.tile(np.arange(GRID_W, dtype=np.float64), rows)
    freqs = ROPE_THETA ** (-np.arange(0, axis_dim, 2, dtype=np.float64) / axis_dim)
    ang = np.concatenate([row_idx[:, None] * freqs, col_idx[:, None] * freqs], axis=-1)
    cos = np.repeat(np.cos(ang), 2, axis=-1)
    sin = np.repeat(np.sin(ang), 2, axis=-1)
    even = (np.arange(hd) % 2 == 0)[None, :]
    as_const = lambda a: jnp.asarray(a.astype(np.float32))
    return as_const(cos), as_const(np.where(even, -sin, 0.0)), as_const(np.where(even, 0.0, sin))


def _project(h, w_in, q_g, k_g, rope, layer, dims):
    t, d = h.shape
    hd, aw, kvw, fw = dims["hd"], dims["aw"], dims["kvw"], dims["fw"]
    tn = _tile(math.gcd(math.gcd(aw, kvw), fw), 512)
    rows = _Rows(dims["t_ctx"], dims["dec_seq"], _tile(min(dims["t_ctx"], dims["dec_seq"]), 1024))
    tm = rows.tm
    nq, nk, nv, nu = aw // tn, kvw // tn, kvw // tn, fw // tn
    cos, se, so = rope
    kern = functools.partial(_proj_kernel, nq=nq, nk=nk, nv=nv, n_ctx_tiles=rows.n_ctx_tiles,
                             hd=hd, q_scale=float(hd) ** -0.5)

    def clampj(lo, n):
        return lambda i, j: (i, jnp.clip(j - lo, 0, n - 1))

    rope_spec = pl.BlockSpec((tm, hd), lambda i, j: (rows.seq_tile(i), 0))
    return pl.pallas_call(
        kern,
        grid=(t // tm, nq + nk + nv + nu),
        in_specs=[
            pl.BlockSpec((tm, d), lambda i, j: (i, 0)),
            pl.BlockSpec((None, d, tn), lambda i, j: (layer, 0, j)),
            pl.BlockSpec((None, 1, hd), lambda i, j: (layer, 0, 0)),
            pl.BlockSpec((None, 1, hd), lambda i, j: (layer, 0, 0)),
            rope_spec, rope_spec, rope_spec,
        ],
        out_specs=[
            pl.BlockSpec((tm, tn), clampj(0, nq)),
            pl.BlockSpec((tm, tn), clampj(nq, nk)),
            pl.BlockSpec((tm, tn), clampj(nq + nk, nv)),
            pl.BlockSpec((tm, tn), clampj(nq + nk + nv, nu)),
        ],
        out_shape=[
            jax.ShapeDtypeStruct((t, aw), BF16),
            jax.ShapeDtypeStruct((t, kvw), F32),
            jax.ShapeDtypeStruct((t, kvw), F32),
            jax.ShapeDtypeStruct((t, fw), BF16),
        ],
        compiler_params=_cp("arbitrary", "arbitrary"),
        name="in_projection",
    )(h, w_in, q_g, k_g, cos, se, so)


def _cache_store_kernel(k_ref, v_ref, kc_in_ref, vc_in_ref, kc_ref, vc_ref):
    kc_ref[...] = k_ref[...].reshape(kc_ref.shape)
    vc_ref[...] = v_ref[...].reshape(vc_ref.shape)


def _store_context_cache(k, v, kc, vc, layer, dims):
    seq, t_ctx = dims["seq"], dims["t_ctx"]
    kvw = k.shape[1]
    tm = _tile(t_ctx // seq, 4) * seq
    nb = tm // seq
    out_spec = pl.BlockSpec((nb, None, seq, kvw), lambda i: (i, layer, 0, 0))
    return pl.pallas_call(
        _cache_store_kernel,
        grid=(t_ctx // tm,),
        in_specs=[
            pl.BlockSpec((tm, kvw), lambda i: (i, 0)),
            pl.BlockSpec((tm, kvw), lambda i: (i, 0)),
            pl.BlockSpec(memory_space=pl.ANY),
            pl.BlockSpec(memory_space=pl.ANY),
        ],
        out_specs=[out_spec, out_spec],
        out_shape=[jax.ShapeDtypeStruct(kc.shape, kc.dtype), jax.ShapeDtypeStruct(vc.shape, vc.dtype)],
        input_output_aliases={2: 0, 3: 1},
        compiler_params=_cp("arbitrary"),
        name="context_cache_store",
    )(k, v, kc, vc)


def _softmax_pv(score_blocks, value_blocks):
    m = score_blocks[0].max(axis=-1, keepdims=True)
    for s in score_blocks[1:]:
        m = jnp.maximum(m, s.max(axis=-1, keepdims=True))
    den = None
    out = None
    for s, v in zip(score_blocks, value_blocks):
        p = jnp.exp(s - m)
        ps = p.sum(axis=-1, keepdims=True)
        pv = jnp.dot(p.astype(BF16), v, preferred_element_type=F32)
        den = ps if den is None else den + ps
        out = pv if out is None else out + pv
    return out * (1.0 / den)


def _attn_ctx_kernel(q_ref, k_ref, v_ref, mix_ref, o_ref, *, nkv, qpk, hd):
    s_len = q_ref.shape[0]
    for kv in range(nkv):
        kb = k_ref[:, kv * hd:(kv + 1) * hd].astype(BF16)
        vb = v_ref[:, kv * hd:(kv + 1) * hd].astype(BF16)
        q3 = jnp.concatenate(
            [q_ref[:, (kv * qpk + g) * hd:(kv * qpk + g + 1) * hd] for g in range(qpk)], axis=0)
        sc = lax.dot_general(q3, kb, _NT, preferred_element_type=F32)
        o = _softmax_pv([sc], [vb])
        for g in range(qpk):
            o_ref[:, (kv * qpk + g) * hd:(kv * qpk + g + 1) * hd] = (
                o[g * s_len:(g + 1) * s_len].astype(o_ref.dtype))


def _attn_lat_kernel(q_ref, k_ref, v_ref, ck_ref, cv_ref, mix_ref, o_ref, *, qpk, hd):
    kb = k_ref[...].astype(BF16)
    ckb = ck_ref[...].astype(BF16)
    vb = v_ref[...].astype(BF16)
    cvb = cv_ref[...].astype(BF16)
    for g in range(qpk):
        qg = q_ref[:, g * hd:(g + 1) * hd]
        s1 = lax.dot_general(qg, kb, _NT, preferred_element_type=F32)
        s2 = lax.dot_general(qg, ckb, _NT, preferred_element_type=F32)
        o_ref[:, g * hd:(g + 1) * hd] = _softmax_pv([s1, s2], [vb, cvb]).astype(o_ref.dtype)


def _attention(q, k, v, cache_k4, cache_v4, mix, layer, dims):
    t, aw = q.shape
    hd, nkv, qpk = dims["hd"], dims["nkv"], dims["qpk"]
    seq, dec_seq, t_ctx, past = dims["seq"], dims["dec_seq"], dims["t_ctx"], dims["past"]
    n_ctx = t_ctx // seq
    kvw = nkv * hd
    mix = pl.pallas_call(
        functools.partial(_attn_ctx_kernel, nkv=nkv, qpk=qpk, hd=hd),
        grid=(n_ctx,),
        in_specs=[
            pl.BlockSpec((seq, aw), lambda b: (b, 0)),
            pl.BlockSpec((seq, kvw), lambda b: (b, 0)),
            pl.BlockSpec((seq, kvw), lambda b: (b, 0)),
            pl.BlockSpec(memory_space=pl.ANY),
        ],
        out_specs=pl.BlockSpec((seq, aw), lambda b: (b, 0)),
        out_shape=jax.ShapeDtypeStruct(mix.shape, mix.dtype),
        input_output_aliases={3: 0},
        compiler_params=_cp("arbitrary"),
        name="attention_context",
    )(q, k, v, mix)

    assert t_ctx % dec_seq == 0
    n_lat = (t - t_ctx) // dec_seq
    tq = _tile(dec_seq, 512)
    nqb = dec_seq // tq
    q_off = t_ctx // tq
    k_off = t_ctx // dec_seq
    qw = qpk * hd
    return pl.pallas_call(
        functools.partial(_attn_lat_kernel, qpk=qpk, hd=hd),
        grid=(n_lat, nkv, nqb),
        in_specs=[
            pl.BlockSpec((tq, qw), lambda b, kv, qi: (q_off + b * nqb + qi, kv)),
            pl.BlockSpec((dec_seq, hd), lambda b, kv, qi: (k_off + b, kv)),
            pl.BlockSpec((dec_seq, hd), lambda b, kv, qi: (k_off + b, kv)),
            pl.BlockSpec((None, None, past, hd), lambda b, kv, qi: (b, layer, 0, kv)),
            pl.BlockSpec((None, None, past, hd), lambda b, kv, qi: (b, layer, 0, kv)),
            pl.BlockSpec(memory_space=pl.ANY),
        ],
        out_specs=pl.BlockSpec((tq, qw), lambda b, kv, qi: (q_off + b * nqb + qi, kv)),
        out_shape=jax.ShapeDtypeStruct(mix.shape, mix.dtype),
        input_output_aliases={5: 0},
        compiler_params=_cp("arbitrary", "arbitrary", "arbitrary"),
        name="attention_latent",
    )(q, k, v, cache_k4, cache_v4, mix)


def _fourier_kernel(u_ref, dpos_ref, dch_ref, wf_ref, mix_ref, o_ref, tt_ref, *, ng, gd, norm):
    s_len = u_ref.shape[0]
    for g in range(ng):
        sl = slice(g * gd, (g + 1) * gd)
        t = jnp.dot(u_ref[:, sl], dch_ref[...], preferred_element_type=F32)
        tt_ref[0:s_len, sl] = t[:, :gd].astype(BF16)
        tt_ref[s_len:2 * s_len, sl] = t[:, gd:].astype(BF16)
    f = (jnp.dot(dpos_ref[...], tt_ref[...], preferred_element_type=F32) * norm).astype(BF16)
    for g in range(ng):
        sl = slice(g * gd, (g + 1) * gd)
        o_ref[:, sl] = jnp.dot(f[:, sl], wf_ref[g].astype(BF16),
                               preferred_element_type=F32).astype(o_ref.dtype)


def _dft_tables(n):
    jk = (np.arange(n, dtype=np.int64)[:, None] * np.arange(n, dtype=np.int64)[None, :]) % n
    ang = jk.astype(np.float64) * (2.0 * np.pi / n)
    return np.cos(ang), np.sin(ang)


def _fourier(u, w_fourier, mix, layer, s_len, row_off, n_seq, tables):
    ng, gd = w_fourier.shape[1], w_fourier.shape[2]
    fw = ng * gd
    assert (mix.shape[1] - fw) % fw == 0 and row_off % s_len == 0
    col = (mix.shape[1] - fw) // fw
    dpos, dch = tables
    off = row_off // s_len
    return pl.pallas_call(
        functools.partial(_fourier_kernel, ng=ng, gd=gd, norm=float(s_len * gd) ** -0.5),
        grid=(n_seq,),
        in_specs=[
            pl.BlockSpec((s_len, fw), lambda b: (off + b, 0)),
            pl.BlockSpec((s_len, 2 * s_len), lambda b: (0, 0)),
            pl.BlockSpec((gd, 2 * gd), lambda b: (0, 0)),
            pl.BlockSpec((None, ng, gd, gd), lambda b: (layer, 0, 0, 0)),
            pl.BlockSpec(memory_space=pl.ANY),
        ],
        out_specs=pl.BlockSpec((s_len, fw), lambda b: (off + b, col)),
        out_shape=jax.ShapeDtypeStruct(mix.shape, mix.dtype),
        input_output_aliases={4: 0},
        scratch_shapes=[pltpu.VMEM((2 * s_len, fw), BF16)],
        compiler_params=_cp("arbitrary"),
        name="fourier_mix",
    )(u, dpos, dch, w_fourier, mix)


def _wout_kernel(m_ref, w_ref, x_ref, g_ref, o_ref):
    acc = jnp.dot(m_ref[...], w_ref[...].astype(BF16), preferred_element_type=F32)
    o_ref[...] = x_ref[...] + g_ref[...] * acc


def _out_project(mix, w_out, x, mods, layer, dims):
    t, d = x.shape
    mw = mix.shape[1]
    rows = _Rows(dims["t_ctx"], dims["dec_seq"], _tile(min(dims["t_ctx"], dims["dec_seq"]), 1024))
    tm = rows.tm
    tn = _tile(d, 512)
    return pl.pallas_call(
        _wout_kernel,
        grid=(t // tm, d // tn),
        in_specs=[
            pl.BlockSpec((tm, mw), lambda i, j: (i, 0)),
            pl.BlockSpec((None, mw, tn), lambda i, j: (layer, 0, j)),
            pl.BlockSpec((tm, tn), lambda i, j: (i, j)),
            _mod_spec(rows, layer, 2, tn, col=lambda j: j),
        ],
        out_specs=pl.BlockSpec((tm, tn), lambda i, j: (i, j)),
        out_shape=jax.ShapeDtypeStruct((t, d), F32),
        compiler_params=_cp("arbitrary", "arbitrary"),
        name="out_projection",
    )(mix, w_out, x, mods)


def _top2_sum(a, b, c, d):
    hi1, lo1 = jnp.maximum(a, b), jnp.minimum(a, b)
    hi2, lo2 = jnp.maximum(c, d), jnp.minimum(c, d)
    return jnp.maximum(hi1, hi2) + jnp.maximum(jnp.minimum(hi1, hi2), jnp.maximum(lo1, lo2))


def _norm_router_kernel(x_ref, g_ref, sc_ref, sh_ref, wr_ref, rb_ref, hp_ref, idx_ref, wt_ref,
                        *, n_exp, n_grp):
    hf = _norm_mod(x_ref[...], g_ref[...], sc_ref[...], sh_ref[...])
    half = hf.shape[1] // 2
    hp_ref[...] = _pack_pair(hf[:, :half], hf[:, half:])

    hi = hf.astype(BF16)
    lo = (hf - hi.astype(F32)).astype(BF16)
    w = wr_ref[...]
    whi = w.astype(BF16)
    wlo = (w - whi.astype(F32)).astype(BF16)
    lg = (lax.dot_general(whi, hi, _NT, preferred_element_type=F32)
          + lax.dot_general(whi, lo, _NT, preferred_element_type=F32)
          + lax.dot_general(wlo, hi, _NT, preferred_element_type=F32))
    ex = jnp.exp(lg - lg.max(axis=0, keepdims=True))
    aff = ex * (1.0 / ex.sum(axis=0, keepdims=True))
    sel = aff + rb_ref[...]

    per = n_exp // n_grp
    row = lambda a, e: a[e:e + 1, :]
    assert per == 4, "group score uses a 4-element top-2 network"
    best_s = _top2_sum(*[row(sel, e) for e in range(per)])
    best_g = jnp.zeros_like(best_s, dtype=jnp.int32)
    for g in range(1, n_grp):
        s = _top2_sum(*[row(sel, g * per + e) for e in range(per)])
        upd = s > best_s
        best_g = jnp.where(upd, g, best_g)
        best_s = jnp.where(upd, s, best_s)

    neg = jnp.float32(-jnp.inf)
    picks = []
    for _ in range(2):
        bv = jnp.full_like(best_s, neg)
        bi = jnp.zeros_like(best_g)
        ba = jnp.zeros_like(best_s)
        for e in range(n_exp):
            ok = best_g == (e // per)
            for (pi, _) in picks:
                ok = jnp.logical_and(ok, pi != e)
            val = jnp.where(ok, row(sel, e), neg)
            upd = val > bv
            bv = jnp.where(upd, val, bv)
            bi = jnp.where(upd, e, bi)
            ba = jnp.where(upd, row(aff, e), ba)
        picks.append((bi, ba))
    (i0, a0), (i1, a1) = picks
    inv = 1.0 / (a0 + a1)
    idx_ref[0:1, :] = i0
    idx_ref[1:2, :] = i1
    wt_ref[0:1, :] = a0 * inv
    wt_ref[1:2, :] = a1 * inv


def _norm_router(x, gain, mods, w_router_t, router_bias, layer, dims):
    t, d = x.shape
    n_exp = w_router_t.shape[0]
    rows = _Rows(dims["t_ctx"], dims["dec_seq"], _tile(min(dims["t_ctx"], dims["dec_seq"]), 256))
    tm = rows.tm
    return pl.pallas_call(
        functools.partial(_norm_router_kernel, n_exp=n_exp, n_grp=N_EXPERT_GROUPS),
        grid=(t // tm,),
        in_specs=[
            pl.BlockSpec((tm, d), lambda i: (i, 0)),
            pl.BlockSpec((None, 1, d), lambda i: (layer, 0, 0)),
            _mod_spec(rows, layer, 4, d),
            _mod_spec(rows, layer, 3, d),
            pl.BlockSpec((n_exp, d), lambda i: (0, 0)),
            pl.BlockSpec((n_exp, 1), lambda i: (0, 0)),
        ],
        out_specs=[
            pl.BlockSpec((tm, d // 2), lambda i: (i, 0)),
            pl.BlockSpec((2, tm), lambda i: (0, i)),
            pl.BlockSpec((2, tm), lambda i: (0, i)),
        ],
        out_shape=[
            jax.ShapeDtypeStruct((t, d // 2), U32),
            jax.ShapeDtypeStruct((2, t), jnp.int32),
            jax.ShapeDtypeStruct((2, t), F32),
        ],
        compiler_params=_cp("arbitrary"),
        name="norm_router",
    )(x, gain, mods, mods, w_router_t, router_bias.reshape(n_exp, 1))


def _dispatch_plan(idx, n_exp, item_rows, sub_rows, n_items):
    t = idx.shape[1]
    e_a = idx.reshape(-1)
    onehot = (e_a[:, None] == jnp.arange(n_exp, dtype=jnp.int32)[None, :]).astype(jnp.int32)
    csum = jnp.cumsum(onehot, axis=0)
    counts = csum[-1]
    rank = jnp.take_along_axis(csum, e_a[:, None], axis=1)[:, 0] - 1
    items_e = (counts + item_rows - 1) // item_rows
    item_end = jnp.cumsum(items_e)
    item_start = item_end - items_e
    pos = item_start[e_a] * item_rows + rank
    tok = jnp.tile(jnp.arange(t, dtype=jnp.int32), 2)
    slot_tok = jnp.zeros((n_items * item_rows,), jnp.int32).at[pos].set(tok)
    w = jnp.arange(n_items, dtype=jnp.int32)
    item_exp = jnp.minimum(jnp.searchsorted(item_end, w, side="right"), n_exp - 1).astype(jnp.int32)
    rows = jnp.clip(counts[item_exp] - (w - item_start[item_exp]) * item_rows, 0, item_rows)
    n_sub = jnp.where(w < item_end[-1], (rows + sub_rows - 1) // sub_rows, 0).astype(jnp.int32)
    n_valid = item_end[-1].astype(jnp.int32).reshape(1)
    return slot_tok, pos.reshape(2, t), item_exp, n_sub, n_valid


def _row_copy(src_hbm, dst, sem, src_row, dst_row, n=1):
    return pltpu.make_async_copy(src_hbm.at[pl.ds(src_row, n)], dst.at[pl.ds(dst_row, n)], sem)


GATHER_UNROLL = 8


def _start_row_gather(src_hbm, dst, sem, idx_ref, n_rows):
    def issue(g, carry):
        base = g * GATHER_UNROLL
        for u in range(GATHER_UNROLL):
            _row_copy(src_hbm, dst, sem, idx_ref[0, 0, base + u], base + u).start()
        return carry

    lax.fori_loop(0, n_rows // GATHER_UNROLL, issue, 0)


def _zero_tail(o_ref, first_sub, n_sub_total, sub):
    def body(s, carry):
        o_ref[pl.ds(pl.multiple_of(s * sub, sub), sub), :] = jnp.zeros((sub, o_ref.shape[1]), o_ref.dtype)
        return carry

    lax.fori_loop(first_sub, n_sub_total, body, 0)


def _for_live_rows(n_sub, sub, group, fn):
    big = group * sub
    n_big = n_sub // group

    def big_body(s, carry):
        fn(pl.ds(pl.multiple_of(s * big, big), big))
        return carry

    lax.fori_loop(0, n_big, big_body, 0)

    def small_body(s, carry):
        fn(pl.ds(pl.multiple_of(s * sub, sub), sub))
        return carry

    lax.fori_loop(n_big * group, n_sub, small_body, 0)


def _ffn_up_kernel(te_ref, ns_ref, nv_ref, tok_ref, tok_next_ref, hp_ref, w1_ref, w3_ref, o_ref,
                   xbuf, xl_ref, xr_ref, wb1_ref, wb3_ref, sem, *, sub, group):
    w = pl.program_id(0)
    c = pl.program_id(1)
    nv = nv_ref[0]
    n_sub_total = o_ref.shape[0] // sub
    half = xl_ref.shape[1]

    def start_gather(toks, n_sub):
        _start_row_gather(hp_ref, xbuf, sem, toks, n_sub * sub)

    @pl.when(jnp.logical_and(w == 0, c == 0))
    def _():
        start_gather(tok_ref, ns_ref[0])

    @pl.when(jnp.logical_and(c == 0, w < nv))
    def _():
        def land(s, carry):
            _row_copy(hp_ref, xbuf, sem, 0, 0, sub).wait()
            return carry

        lax.fori_loop(0, ns_ref[w], land, 0)

        def unpack(s, carry):
            rows = pl.ds(pl.multiple_of(s * sub, sub), sub)
            lo, hi = _unpack_pair(xbuf[rows, :])
            xl_ref[rows, :] = lo.astype(BF16)
            xr_ref[rows, :] = hi.astype(BF16)
            return carry

        lax.fori_loop(0, ns_ref[w], unpack, 0)

        @pl.when(w + 1 < nv)
        def _():
            start_gather(tok_next_ref, ns_ref[jnp.minimum(w + 1, ns_ref.shape[0] - 1)])

    @pl.when(w < nv)
    def _():
        wb1_ref[...] = w1_ref[...].astype(BF16)
        wb3_ref[...] = w3_ref[...].astype(BF16)

        def body(rows):
            xl = xl_ref[rows, :]
            xr = xr_ref[rows, :]
            a = (jnp.dot(xl, wb1_ref[:half, :], preferred_element_type=F32)
                 + jnp.dot(xr, wb1_ref[half:, :], preferred_element_type=F32))
            b = (jnp.dot(xl, wb3_ref[:half, :], preferred_element_type=F32)
                 + jnp.dot(xr, wb3_ref[half:, :], preferred_element_type=F32))
            o_ref[rows, :] = (a * jax.nn.sigmoid(a) * b).astype(o_ref.dtype)

        _for_live_rows(ns_ref[w], sub, group, body)
        _zero_tail(o_ref, ns_ref[w], n_sub_total, sub)

    @pl.when(w >= nv)
    def _():
        o_ref[...] = jnp.zeros_like(o_ref)


def _ffn_down_kernel(te_ref, ns_ref, nv_ref, h_ref, wl_ref, wr_ref, o_ref, wbl_ref, wbr_ref,
                     *, sub, group):
    w = pl.program_id(0)
    n_sub_total = o_ref.shape[0] // sub

    @pl.when(w < nv_ref[0])
    def _():
        wbl_ref[...] = wl_ref[...].astype(BF16)
        wbr_ref[...] = wr_ref[...].astype(BF16)

        def body(rows):
            h = h_ref[rows, :]
            yl = jnp.dot(h, wbl_ref[...], preferred_element_type=F32)
            yr = jnp.dot(h, wbr_ref[...], preferred_element_type=F32)
            o_ref[rows, :] = _pack_pair(yl, yr)

        _for_live_rows(ns_ref[w], sub, group, body)
        _zero_tail(o_ref, ns_ref[w], n_sub_total, sub)

    @pl.when(w >= nv_ref[0])
    def _():
        o_ref[...] = jnp.zeros_like(o_ref)


def _expert_ffn(hp, slot_tok, item_exp, n_sub, n_valid, w1, w3, w2, layer, item_rows, sub_rows):
    t, dh = hp.shape
    d = 2 * dh
    f = w1.shape[3]
    n_items = item_exp.shape[0]
    r_total = n_items * item_rows
    tf = _tile(f, 256)
    nf = f // tf
    toks = slot_tok.reshape(n_items, 1, item_rows)

    def live(w, nv):
        return jnp.minimum(w, nv[0] - 1)

    def chunk(w, c, nv, n_chunks):
        return jnp.where(w < nv[0], c, n_chunks - 1)

    hid = pl.pallas_call(
        functools.partial(_ffn_up_kernel, sub=sub_rows, group=MOE_GROUP_SUBS),
        grid_spec=pltpu.PrefetchScalarGridSpec(
            num_scalar_prefetch=3,
            grid=(n_items, nf),
            in_specs=[
                pl.BlockSpec((1, 1, item_rows), lambda w, c, te, ns, nv: (w, 0, 0),
                             memory_space=pltpu.SMEM),
                pl.BlockSpec((1, 1, item_rows),
                             lambda w, c, te, ns, nv: (jnp.minimum(w + 1, n_items - 1), 0, 0),
                             memory_space=pltpu.SMEM),
                pl.BlockSpec(memory_space=pl.ANY),
                pl.BlockSpec((None, None, d, tf),
                             lambda w, c, te, ns, nv: (layer, te[live(w, nv)], 0, chunk(w, c, nv, nf))),
                pl.BlockSpec((None, None, d, tf),
                             lambda w, c, te, ns, nv: (layer, te[live(w, nv)], 0, chunk(w, c, nv, nf))),
            ],
            out_specs=pl.BlockSpec((item_rows, tf), lambda w, c, te, ns, nv: (w, c)),
            scratch_shapes=[
                pltpu.VMEM((item_rows, dh), U32),
                pltpu.VMEM((item_rows, dh), BF16),
                pltpu.VMEM((item_rows, dh), BF16),
                pltpu.VMEM((d, tf), BF16),
                pltpu.VMEM((d, tf), BF16),
                pltpu.SemaphoreType.DMA,
            ],
        ),
        out_shape=jax.ShapeDtypeStruct((r_total, f), BF16),
        compiler_params=_cp("arbitrary", "arbitrary"),
        name="expert_ffn_up",
    )(item_exp, n_sub, n_valid, toks, toks, hp, w1, w3)

    tn = _tile(dh, 512)
    nn = dh // tn
    return pl.pallas_call(
        functools.partial(_ffn_down_kernel, sub=sub_rows, group=MOE_GROUP_SUBS),
        grid_spec=pltpu.PrefetchScalarGridSpec(
            num_scalar_prefetch=3,
            grid=(n_items, nn),
            in_specs=[
                pl.BlockSpec((item_rows, f), lambda w, c, te, ns, nv: (live(w, nv), 0)),
                pl.BlockSpec((None, None, f, tn),
                             lambda w, c, te, ns, nv: (layer, te[live(w, nv)], 0, chunk(w, c, nv, nn))),
                pl.BlockSpec((None, None, f, tn),
                             lambda w, c, te, ns, nv: (layer, te[live(w, nv)], 0,
                                                       nn + chunk(w, c, nv, nn))),
            ],
            out_specs=pl.BlockSpec((item_rows, tn), lambda w, c, te, ns, nv: (w, c)),
            scratch_shapes=[pltpu.VMEM((f, tn), BF16), pltpu.VMEM((f, tn), BF16)],
        ),
        out_shape=jax.ShapeDtypeStruct((r_total, dh), U32),
        compiler_params=_cp("arbitrary", "arbitrary"),
        name="expert_ffn_down",
    )(item_exp, n_sub, n_valid, hid, w2, w2)


def _combine_kernel(pos_ref, pos_next_ref, y_ref, x_ref, g_ref, wt_ref, *rest, tm, with_next_norm,
                    n_ctx_tiles):
    if with_next_norm:
        gn_ref, scn_ref, shn_ref, o_ref, h_ref, ybuf, sem = rest
    else:
        o_ref, o_lat_ref, ybuf, sem = rest
    i = pl.program_id(0)
    slot = i % 2

    def start_gather(poss, dst_slot):
        _start_row_gather(y_ref, ybuf.at[dst_slot], sem.at[dst_slot], poss, 2 * tm)

    @pl.when(i == 0)
    def _():
        start_gather(pos_ref, 0)

    @pl.when(i + 1 < pl.num_programs(0))
    def _():
        start_gather(pos_next_ref, 1 - slot)

    _row_copy(y_ref, ybuf.at[slot], sem.at[slot], 0, 0, 2 * tm).wait()

    half = ybuf.shape[2]
    w0 = wt_ref[:, 0:1]
    w1 = wt_ref[:, 1:2]
    l0, r0 = _unpack_pair(ybuf[slot, 0:tm, :])
    l1, r1 = _unpack_pair(ybuf[slot, tm:2 * tm, :])
    ol = x_ref[:, :half] + g_ref[:, :half] * (w0 * l0 + w1 * l1)
    orr = x_ref[:, half:] + g_ref[:, half:] * (w0 * r0 + w1 * r1)

    def store(dst_ref):
        dst_ref[:, :half] = ol
        dst_ref[:, half:] = orr

    if with_next_norm:
        store(o_ref)
    else:
        @pl.when(i < n_ctx_tiles)
        def _():
            store(o_ref)

        @pl.when(i >= n_ctx_tiles)
        def _():
            store(o_lat_ref)

    if with_next_norm:
        ms = (jnp.sum(ol * ol, axis=-1, keepdims=True)
              + jnp.sum(orr * orr, axis=-1, keepdims=True)) * (1.0 / (2 * half))
        inv = lax.rsqrt(ms + EPS)
        for sl, part in ((slice(0, half), ol), (slice(half, 2 * half), orr)):
            yn = part * inv * gn_ref[:, sl]
            h_ref[:, sl] = (yn * (1.0 + scn_ref[:, sl]) + shn_ref[:, sl]).astype(h_ref.dtype)


def _combine(y, pos, wts, x, mods, layer, dims, next_gain=None):
    t, d = x.shape
    rows = _Rows(dims["t_ctx"], dims["dec_seq"], _tile(min(dims["t_ctx"], dims["dec_seq"]), 256))
    tm = rows.tm
    nt = t // tm
    with_next = next_gain is not None
    pos_tiles = pos.reshape(2, nt, tm).transpose(1, 0, 2).reshape(nt, 1, 2 * tm)
    row_spec = pl.BlockSpec((tm, d), lambda i: (i, 0))
    in_specs = [
        pl.BlockSpec((1, 1, 2 * tm), lambda i: (i, 0, 0), memory_space=pltpu.SMEM),
        pl.BlockSpec((1, 1, 2 * tm), lambda i: (jnp.minimum(i + 1, nt - 1), 0, 0),
                     memory_space=pltpu.SMEM),
        pl.BlockSpec(memory_space=pl.ANY),
        row_spec,
        _mod_spec(rows, layer, 5, d),
        pl.BlockSpec((tm, 2), lambda i: (i, 0)),
    ]
    args = [pos_tiles, pos_tiles, y, x, mods, wts.T]
    nct = rows.n_ctx_tiles
    if with_next:
        in_specs += [pl.BlockSpec((None, 1, d), lambda i: (layer + 1, 0, 0)),
                     _mod_spec(rows, layer + 1, 1, d), _mod_spec(rows, layer + 1, 0, d)]
        args += [next_gain, mods, mods]
        out_specs = [row_spec, row_spec]
        out_shape = [jax.ShapeDtypeStruct((t, d), F32), jax.ShapeDtypeStruct((t, d), BF16)]
    else:
        out_specs = [pl.BlockSpec((tm, d), lambda i: (jnp.minimum(i, nct - 1), 0)),
                     pl.BlockSpec((tm, d), lambda i: (jnp.maximum(i - nct, 0), 0))]
        out_shape = [jax.ShapeDtypeStruct((dims["t_ctx"], d), F32),
                     jax.ShapeDtypeStruct((t - dims["t_ctx"], d), F32)]
    return pl.pallas_call(
        functools.partial(_combine_kernel, tm=tm, with_next_norm=with_next, n_ctx_tiles=nct),
        grid=(nt,),
        in_specs=in_specs,
        out_specs=out_specs,
        out_shape=out_shape,
        scratch_shapes=[pltpu.VMEM((2, 2 * tm, d // 2), U32), pltpu.SemaphoreType.DMA((2,))],
        compiler_params=_cp("arbitrary"),
        name="moe_combine",
    )(*args)


MOE_ITEM_ROWS = 1536
MOE_SUB_ROWS = 128
MOE_GROUP_SUBS = 4


def kernel(x_prompt, x_sample, cache_k, cache_v, c, c_ctx, w_ada, b_ada, norm_mix, norm_ffn, w_in, q_norm, k_norm, w_fourier, w_out, w_router, router_bias, w1, w3, w2):
    batch, seq, d = x_prompt.shape
    dec_batch, dec_seq, _ = x_sample.shape
    depth = w_in.shape[0]
    past, nkv, hd = cache_k.shape[2], cache_k.shape[3], cache_k.shape[4]
    ng, gd = w_fourier.shape[1], w_fourier.shape[2]
    fw = ng * gd
    kvw = nkv * hd
    aw = w_out.shape[1] - fw
    n_exp = w_router.shape[1]
    t_ctx = batch * seq
    t_lat = dec_batch * dec_seq
    t = t_ctx + t_lat
    assert hd == LANE and dec_batch + 1 <= COND_ROWS and w_in.shape[2] == aw + 2 * kvw + fw
    dims = dict(hd=hd, aw=aw, kvw=kvw, fw=fw, nkv=nkv, qpk=aw // kvw, seq=seq, dec_seq=dec_seq,
                t_ctx=t_ctx, past=past)

    cond = jnp.zeros((COND_ROWS, d), F32).at[0].set(c_ctx).at[1:1 + dec_batch].set(c)
    mods = _modulation(cond, w_ada, b_ada).reshape(depth, COND_ROWS, N_MOD, 1, d)

    rope = _rope_tables(dec_seq, hd)
    cs_c, sn_c = _dft_tables(seq)
    cs_l, sn_l = _dft_tables(dec_seq)
    cs_g, sn_g = _dft_tables(gd)
    bf_const = lambda a: jnp.asarray(a.astype(np.float32).astype(BF16))
    dch = bf_const(np.concatenate([cs_g, sn_g], axis=1))
    dft_ctx = (bf_const(np.concatenate([cs_c, -sn_c], axis=1)), dch)
    dft_lat = (bf_const(np.concatenate([cs_l, -sn_l], axis=1)), dch)

    cache_k4 = cache_k.reshape(dec_batch, depth, past, kvw)
    cache_v4 = cache_v.reshape(dec_batch, depth, past, kvw)
    w_router_t = w_router.T
    n_moe_items = (2 * t) // MOE_ITEM_ROWS + n_exp
    mix = jnp.zeros((t, aw + fw), BF16)
    norm_mix3 = norm_mix.reshape(depth, 1, d)
    norm_ffn3 = norm_ffn.reshape(depth, 1, d)
    q_norm3 = q_norm.reshape(depth, 1, hd)
    k_norm3 = k_norm.reshape(depth, 1, hd)

    x, h = _stack_norm_modulate(x_prompt.reshape(t_ctx, d), x_sample.reshape(t_lat, d),
                                norm_mix3, mods, 0, dec_seq)
    new_k = jnp.zeros((batch, depth, seq, kvw), F32)
    new_v = jnp.zeros((batch, depth, seq, kvw), F32)
    for l in range(depth):
        q, k, v, u = _project(h, w_in, q_norm3, k_norm3, rope, l, dims)
        new_k, new_v = _store_context_cache(k, v, new_k, new_v, l, dims)
        mix = _attention(q, k, v, cache_k4, cache_v4, mix, l, dims)
        mix = _fourier(u, w_fourier, mix, l, seq, 0, batch, dft_ctx)
        mix = _fourier(u, w_fourier, mix, l, dec_seq, t_ctx, dec_batch, dft_lat)
        x = _out_project(mix, w_out, x, mods, l, dims)

        hp, idx, wts = _norm_router(x, norm_ffn3, mods, w_router_t, router_bias, l, dims)
        slot_tok, pos, item_exp, n_sub, n_valid = _dispatch_plan(
            idx, n_exp, MOE_ITEM_ROWS, MOE_SUB_ROWS, n_moe_items)
        y = _expert_ffn(hp, slot_tok, item_exp, n_sub, n_valid, w1, w3, w2, l,
                        MOE_ITEM_ROWS, MOE_SUB_ROWS)
        if l + 1 < depth:
            x, h = _combine(y, pos, wts, x, mods, l, dims, next_gain=norm_mix3)
        else:
            y_ctx, y_lat = _combine(y, pos, wts, x, mods, l, dims)

    y_prompt = y_ctx.reshape(batch, seq, d)
    y_sample = y_lat.reshape(dec_batch, dec_seq, d)
    cache_shape = (batch, depth, seq, nkv, hd)
    return (y_prompt, y_sample, new_k.reshape(cache_shape), new_v.reshape(cache_shape))
```

```python
import functools
import math

import jax
import jax.numpy as jnp
import numpy as np
from jax import lax
from jax.experimental import pallas as pl
from jax.experimental.pallas import tpu as pltpu

GRID_W = 64
N_EXPERT_GROUPS = 4
ROPE_THETA = 10000.0
EPS = 1e-6
N_MOD = 6
COND_ROWS = 8

VMEM_LIMIT_BYTES = 56 * 1024 * 1024
LANE = 128

BF16 = jnp.bfloat16
F32 = jnp.float32
U32 = jnp.uint32

_NT = (((1,), (1,)), ((), ()))


def _cp(*sem):
    return pltpu.CompilerParams(dimension_semantics=sem, vmem_limit_bytes=VMEM_LIMIT_BYTES)


def _tile(n, pref):
    t = min(n, pref)
    while n % t:
        t -= 1
    return t


def _pack_pair(a, b):
    ua = lax.bitcast_convert_type(a.astype(BF16).astype(F32), U32)
    ub = lax.bitcast_convert_type(b.astype(BF16).astype(F32), U32)
    return (ua >> 16) | (ub & jnp.uint32(0xFFFF0000))


def _unpack_pair(w):
    lo = lax.bitcast_convert_type(w << 16, F32)
    hi = lax.bitcast_convert_type(w & jnp.uint32(0xFFFF0000), F32)
    return lo, hi


def _mod_kernel(c_ref, w_ref, b_ref, o_ref):
    c = c_ref[...]
    s = (c * jax.nn.sigmoid(c)).astype(BF16)
    o_ref[...] = jnp.dot(s, w_ref[...].astype(BF16), preferred_element_type=F32) + b_ref[...]


def _modulation(cond, w_ada, b_ada):
    depth, d, n = w_ada.shape
    tn = _tile(n, 512)
    return pl.pallas_call(
        _mod_kernel,
        grid=(depth, n // tn),
        in_specs=[
            pl.BlockSpec((COND_ROWS, d), lambda l, j: (0, 0)),
            pl.BlockSpec((None, d, tn), lambda l, j: (l, 0, j)),
            pl.BlockSpec((None, 1, tn), lambda l, j: (l, 0, j)),
        ],
        out_specs=pl.BlockSpec((None, COND_ROWS, tn), lambda l, j: (l, 0, j)),
        out_shape=jax.ShapeDtypeStruct((depth, COND_ROWS, n), F32),
        compiler_params=_cp("arbitrary", "arbitrary"),
        name="modulation",
    )(cond, w_ada, b_ada.reshape(depth, 1, n))


class _Rows:
    def __init__(self, t_ctx, dec_seq, tm):
        assert t_ctx % tm == 0 and dec_seq % tm == 0
        self.tm = tm
        self.n_ctx_tiles = t_ctx // tm
        self.tiles_per_seq = dec_seq // tm

    def cond(self, i):
        lat = 1 + (i - self.n_ctx_tiles) // self.tiles_per_seq
        return jnp.where(i < self.n_ctx_tiles, 0, lat)

    def seq_tile(self, i):
        return jnp.maximum(i - self.n_ctx_tiles, 0) % self.tiles_per_seq


def _mod_spec(rows, layer, which, width, col=None):
    if col is None:
        return pl.BlockSpec((None, None, None, 1, width),
                            lambda i, *_: (layer, rows.cond(i), which, 0, 0))
    return pl.BlockSpec((None, None, None, 1, width),
                        lambda i, j, *_: (layer, rows.cond(i), which, 0, col(j)))


def _norm_mod(x, g, sc, sh):
    ms = jnp.mean(x * x, axis=-1, keepdims=True)
    y = x * lax.rsqrt(ms + EPS) * g
    return y * (1.0 + sc) + sh


def _stack_norm_mod_kernel(xc_ref, xl_ref, g_ref, sc_ref, sh_ref, x_ref, h_ref, *, n_ctx_tiles):
    def emit(src_ref):
        x = src_ref[...]
        x_ref[...] = x
        h_ref[...] = _norm_mod(x, g_ref[...], sc_ref[...], sh_ref[...]).astype(h_ref.dtype)

    @pl.when(pl.program_id(0) < n_ctx_tiles)
    def _():
        emit(xc_ref)

    @pl.when(pl.program_id(0) >= n_ctx_tiles)
    def _():
        emit(xl_ref)


def _stack_norm_modulate(x_ctx, x_lat, gain, mods, layer, dec_seq):
    t_ctx, d = x_ctx.shape
    t = t_ctx + x_lat.shape[0]
    rows = _Rows(t_ctx, dec_seq, _tile(min(t_ctx, dec_seq), 256))
    tm = rows.tm
    nct = rows.n_ctx_tiles
    row_spec = pl.BlockSpec((tm, d), lambda i: (i, 0))
    return pl.pallas_call(
        functools.partial(_stack_norm_mod_kernel, n_ctx_tiles=nct),
        grid=(t // tm,),
        in_specs=[
            pl.BlockSpec((tm, d), lambda i: (jnp.minimum(i, nct - 1), 0)),
            pl.BlockSpec((tm, d), lambda i: (jnp.maximum(i - nct, 0), 0)),
            pl.BlockSpec((None, 1, d), lambda i: (layer, 0, 0)),
            _mod_spec(rows, layer, 1, d),
            _mod_spec(rows, layer, 0, d),
        ],
        out_specs=[row_spec, row_spec],
        out_shape=[jax.ShapeDtypeStruct((t, d), F32), jax.ShapeDtypeStruct((t, d), BF16)],
        compiler_params=_cp("arbitrary"),
        name="stack_norm_modulate",
    )(x_ctx, x_lat, gain, mods, mods)


def _proj_kernel(h_ref, w_ref, qg_ref, kg_ref, cos_ref, se_ref, so_ref,
                 q_ref, k_ref, v_ref, u_ref, *, nq, nk, nv, n_ctx_tiles, hd, q_scale):
    i = pl.program_id(0)
    j = pl.program_id(1)
    acc = jnp.dot(h_ref[...], w_ref[...].astype(BF16), preferred_element_type=F32)
    heads = acc.shape[1] // hd

    def head_norm(gain_ref, out_ref, mult, rope):
        for hh in range(heads):
            sl = slice(hh * hd, (hh + 1) * hd)
            blk = acc[:, sl]
            ms = jnp.mean(blk * blk, axis=-1, keepdims=True)
            y = blk * lax.rsqrt(ms + EPS) * gain_ref[...]
            if rope:
                y = (y * cos_ref[...] + pltpu.roll(y, hd - 1, 1) * se_ref[...]
                     + pltpu.roll(y, 1, 1) * so_ref[...])
            if mult != 1.0:
                y = y * mult
            out_ref[:, sl] = y.astype(out_ref.dtype)

    def qk_path(gain_ref, out_ref, mult):
        @pl.when(i < n_ctx_tiles)
        def _():
            head_norm(gain_ref, out_ref, mult, False)

        @pl.when(i >= n_ctx_tiles)
        def _():
            head_norm(gain_ref, out_ref, mult, True)

    @pl.when(j < nq)
    def _():
        qk_path(qg_ref, q_ref, q_scale)

    @pl.when(jnp.logical_and(j >= nq, j < nq + nk))
    def _():
        qk_path(kg_ref, k_ref, 1.0)

    @pl.when(jnp.logical_and(j >= nq + nk, j < nq + nk + nv))
    def _():
        v_ref[...] = acc

    @pl.when(j >= nq + nk + nv)
    def _():
        u_ref[...] = acc.astype(u_ref.dtype)


def _rope_tables(n_tok, hd):
    axis_dim = hd // 2
    rows = n_tok // GRID_W
    row_idx = np.repeat(np.arange(rows, dtype=np.float64), GRID_W)
    col_idx = np.tile(np.arange(GRID_W, dtype=np.float64), rows)
    freqs = ROPE_THETA ** (-np.arange(0, axis_dim, 2, dtype=np.float64) / axis_dim)
    ang = np.concatenate([row_idx[:, None] * freqs, col_idx[:, None] * freqs], axis=-1)
    cos = np.repeat(np.cos(ang), 2, axis=-1)
    sin = np.repeat(np.sin(ang), 2, axis=-1)
    even = (np.arange(hd) % 2 == 0)[None, :]
    as_const = lambda a: jnp.asarray(a.astype(np.float32))
    return as_const(cos), as_const(np.where(even, -sin, 0.0)), as_const(np.where(even, 0.0, sin))


def _project(h, w_in, q_g, k_g, rope, layer, dims):
    t, d = h.shape
    hd, aw, kvw, fw = dims["hd"], dims["aw"], dims["kvw"], dims["fw"]
    tn = _tile(math.gcd(math.gcd(aw, kvw), fw), 512)
    rows = _Rows(dims["t_ctx"], dims["dec_seq"], _tile(min(dims["t_ctx"], dims["dec_seq"]), 1024))
    tm = rows.tm
    nq, nk, nv, nu = aw // tn, kvw // tn, kvw // tn, fw // tn
    cos, se, so = rope
    kern = functools.partial(_proj_kernel, nq=nq, nk=nk, nv=nv, n_ctx_tiles=rows.n_ctx_tiles,
                             hd=hd, q_scale=float(hd) ** -0.5)

    def clampj(lo, n):
        return lambda i, j: (i, jnp.clip(j - lo, 0, n - 1))

    rope_spec = pl.BlockSpec((tm, hd), lambda i, j: (rows.seq_tile(i), 0))
    return pl.pallas_call(
        kern,
        grid=(t // tm, nq + nk + nv + nu),
        in_specs=[
            pl.BlockSpec((tm, d), lambda i, j: (i, 0)),
            pl.BlockSpec((None, d, tn), lambda i, j: (layer, 0, j)),
            pl.BlockSpec((None, 1, hd), lambda i, j: (layer, 0, 0)),
            pl.BlockSpec((None, 1, hd), lambda i, j: (layer, 0, 0)),
            rope_spec, rope_spec, rope_spec,
        ],
        out_specs=[
            pl.BlockSpec((tm, tn), clampj(0, nq)),
            pl.BlockSpec((tm, tn), clampj(nq, nk)),
            pl.BlockSpec((tm, tn), clampj(nq + nk, nv)),
            pl.BlockSpec((tm, tn), clampj(nq + nk + nv, nu)),
        ],
        out_shape=[
            jax.ShapeDtypeStruct((t, aw), BF16),
            jax.ShapeDtypeStruct((t, kvw), F32),
            jax.ShapeDtypeStruct((t, kvw), F32),
            jax.ShapeDtypeStruct((t, fw), BF16),
        ],
        compiler_params=_cp("arbitrary", "arbitrary"),
        name="in_projection",
    )(h, w_in, q_g, k_g, cos, se, so)


def _cache_store_kernel(k_ref, v_ref, kc_in_ref, vc_in_ref, kc_ref, vc_ref):
    kc_ref[...] = k_ref[...].reshape(kc_ref.shape)
    vc_ref[...] = v_ref[...].reshape(vc_ref.shape)


def _store_context_cache(k, v, kc, vc, layer, dims):
    seq, t_ctx = dims["seq"], dims["t_ctx"]
    kvw = k.shape[1]
    tm = _tile(t_ctx // seq, 4) * seq
    nb = tm // seq
    out_spec = pl.BlockSpec((nb, None, seq, kvw), lambda i: (i, layer, 0, 0))
    return pl.pallas_call(
        _cache_store_kernel,
        grid=(t_ctx // tm,),
        in_specs=[
            pl.BlockSpec((tm, kvw), lambda i: (i, 0)),
            pl.BlockSpec((tm, kvw), lambda i: (i, 0)),
            pl.BlockSpec(memory_space=pl.ANY),
            pl.BlockSpec(memory_space=pl.ANY),
        ],
        out_specs=[out_spec, out_spec],
        out_shape=[jax.ShapeDtypeStruct(kc.shape, kc.dtype), jax.ShapeDtypeStruct(vc.shape, vc.dtype)],
        input_output_aliases={2: 0, 3: 1},
        compiler_params=_cp("arbitrary"),
        name="context_cache_store",
    )(k, v, kc, vc)


def _softmax_pv(score_blocks, value_blocks):
    m = score_blocks[0].max(axis=-1, keepdims=True)
    for s in score_blocks[1:]:
        m = jnp.maximum(m, s.max(axis=-1, keepdims=True))
    den = None
    out = None
    for s, v in zip(score_blocks, value_blocks):
        p = jnp.exp(s - m)
        ps = p.sum(axis=-1, keepdims=True)
        pv = jnp.dot(p.astype(BF16), v, preferred_element_type=F32)
        den = ps if den is None else den + ps
        out = pv if out is None else out + pv
    return out * (1.0 / den)


def _attn_ctx_kernel(q_ref, k_ref, v_ref, mix_ref, o_ref, *, nkv, qpk, hd):
    s_len = q_ref.shape[0]
    for kv in range(nkv):
        kb = k_ref[:, kv * hd:(kv + 1) * hd].astype(BF16)
        vb = v_ref[:, kv * hd:(kv + 1) * hd].astype(BF16)
        q3 = jnp.concatenate(
            [q_ref[:, (kv * qpk + g) * hd:(kv * qpk + g + 1) * hd] for g in range(qpk)], axis=0)
        sc = lax.dot_general(q3, kb, _NT, preferred_element_type=F32)
        o = _softmax_pv([sc], [vb])
        for g in range(qpk):
            o_ref[:, (kv * qpk + g) * hd:(kv * qpk + g + 1) * hd] = (
                o[g * s_len:(g + 1) * s_len].astype(o_ref.dtype))


def _attn_lat_kernel(q_ref, k_ref, v_ref, ck_ref, cv_ref, mix_ref, o_ref, *, qpk, hd):
    kb = k_ref[...].astype(BF16)
    ckb = ck_ref[...].astype(BF16)
    vb = v_ref[...].astype(BF16)
    cvb = cv_ref[...].astype(BF16)
    for g in range(qpk):
        qg = q_ref[:, g * hd:(g + 1) * hd]
        s1 = lax.dot_general(qg, kb, _NT, preferred_element_type=F32)
        s2 = lax.dot_general(qg, ckb, _NT, preferred_element_type=F32)
        o_ref[:, g * hd:(g + 1) * hd] = _softmax_pv([s1, s2], [vb, cvb]).astype(o_ref.dtype)


def _attention(q, k, v, cache_k4, cache_v4, mix, layer, dims):
    t, aw = q.shape
    hd, nkv, qpk = dims["hd"], dims["nkv"], dims["qpk"]
    seq, dec_seq, t_ctx, past = dims["seq"], dims["dec_seq"], dims["t_ctx"], dims["past"]
    n_ctx = t_ctx // seq
    kvw = nkv * hd
    mix = pl.pallas_call(
        functools.partial(_attn_ctx_kernel, nkv=nkv, qpk=qpk, hd=hd),
        grid=(n_ctx,),
        in_specs=[
            pl.BlockSpec((seq, aw), lambda b: (b, 0)),
            pl.BlockSpec((seq, kvw), lambda b: (b, 0)),
            pl.BlockSpec((seq, kvw), lambda b: (b, 0)),
            pl.BlockSpec(memory_space=pl.ANY),
        ],
        out_specs=pl.BlockSpec((seq, aw), lambda b: (b, 0)),
        out_shape=jax.ShapeDtypeStruct(mix.shape, mix.dtype),
        input_output_aliases={3: 0},
        compiler_params=_cp("arbitrary"),
        name="attention_context",
    )(q, k, v, mix)

    assert t_ctx % dec_seq == 0
    n_lat = (t - t_ctx) // dec_seq
    tq = _tile(dec_seq, 512)
    nqb = dec_seq // tq
    q_off = t_ctx // tq
    k_off = t_ctx // dec_seq
    qw = qpk * hd
    return pl.pallas_call(
        functools.partial(_attn_lat_kernel, qpk=qpk, hd=hd),
        grid=(n_lat, nkv, nqb),
        in_specs=[
            pl.BlockSpec((tq, qw), lambda b, kv, qi: (q_off + b * nqb + qi, kv)),
            pl.BlockSpec((dec_seq, hd), lambda b, kv, qi: (k_off + b, kv)),
            pl.BlockSpec((dec_seq, hd), lambda b, kv, qi: (k_off + b, kv)),
            pl.BlockSpec((None, None, past, hd), lambda b, kv, qi: (b, layer, 0, kv)),
            pl.BlockSpec((None, None, past, hd), lambda b, kv, qi: (b, layer, 0, kv)),
            pl.BlockSpec(memory_space=pl.ANY),
        ],
        out_specs=pl.BlockSpec((tq, qw), lambda b, kv, qi: (q_off + b * nqb + qi, kv)),
        out_shape=jax.ShapeDtypeStruct(mix.shape, mix.dtype),
        input_output_aliases={5: 0},
        compiler_params=_cp("arbitrary", "arbitrary", "arbitrary"),
        name="attention_latent",
    )(q, k, v, cache_k4, cache_v4, mix)


def _fourier_kernel(u_ref, dpos_ref, dch_ref, wf_ref, mix_ref, o_ref, tt_ref, *, ng, gd, norm):
    s_len = u_ref.shape[0]
    for g in range(ng):
        sl = slice(g * gd, (g + 1) * gd)
        t = jnp.dot(u_ref[:, sl], dch_ref[...], preferred_element_type=F32)
        tt_ref[0:s_len, sl] = t[:, :gd].astype(BF16)
        tt_ref[s_len:2 * s_len, sl] = t[:, gd:].astype(BF16)
    f = (jnp.dot(dpos_ref[...], tt_ref[...], preferred_element_type=F32) * norm).astype(BF16)
    for g in range(ng):
        sl = slice(g * gd, (g + 1) * gd)
        o_ref[:, sl] = jnp.dot(f[:, sl], wf_ref[g].astype(BF16),
                               preferred_element_type=F32).astype(o_ref.dtype)


def _dft_tables(n):
    jk = (np.arange(n, dtype=np.int64)[:, None] * np.arange(n, dtype=np.int64)[None, :]) % n
    ang = jk.astype(np.float64) * (2.0 * np.pi / n)
    return np.cos(ang), np.sin(ang)


def _fourier(u, w_fourier, mix, layer, s_len, row_off, n_seq, tables):
    ng, gd = w_fourier.shape[1], w_fourier.shape[2]
    fw = ng * gd
    assert (mix.shape[1] - fw) % fw == 0 and row_off % s_len == 0
    col = (mix.shape[1] - fw) // fw
    dpos, dch = tables
    off = row_off // s_len
    return pl.pallas_call(
        functools.partial(_fourier_kernel, ng=ng, gd=gd, norm=float(s_len * gd) ** -0.5),
        grid=(n_seq,),
        in_specs=[
            pl.BlockSpec((s_len, fw), lambda b: (off + b, 0)),
            pl.BlockSpec((s_len, 2 * s_len), lambda b: (0, 0)),
            pl.BlockSpec((gd, 2 * gd), lambda b: (0, 0)),
            pl.BlockSpec((None, ng, gd, gd), lambda b: (layer, 0, 0, 0)),
            pl.BlockSpec(memory_space=pl.ANY),
        ],
        out_specs=pl.BlockSpec((s_len, fw), lambda b: (off + b, col)),
        out_shape=jax.ShapeDtypeStruct(mix.shape, mix.dtype),
        input_output_aliases={4: 0},
        scratch_shapes=[pltpu.VMEM((2 * s_len, fw), BF16)],
        compiler_params=_cp("arbitrary"),
        name="fourier_mix",
    )(u, dpos, dch, w_fourier, mix)


def _wout_kernel(m_ref, w_ref, x_ref, g_ref, o_ref):
    acc = jnp.dot(m_ref[...], w_ref[...].astype(BF16), preferred_element_type=F32)
    o_ref[...] = x_ref[...] + g_ref[...] * acc


def _out_project(mix, w_out, x, mods, layer, dims):
    t, d = x.shape
    mw = mix.shape[1]
    rows = _Rows(dims["t_ctx"], dims["dec_seq"], _tile(min(dims["t_ctx"], dims["dec_seq"]), 1024))
    tm = rows.tm
    tn = _tile(d, 512)
    return pl.pallas_call(
        _wout_kernel,
        grid=(t // tm, d // tn),
        in_specs=[
            pl.BlockSpec((tm, mw), lambda i, j: (i, 0)),
            pl.BlockSpec((None, mw, tn), lambda i, j: (layer, 0, j)),
            pl.BlockSpec((tm, tn), lambda i, j: (i, j)),
            _mod_spec(rows, layer, 2, tn, col=lambda j: j),
        ],
        out_specs=pl.BlockSpec((tm, tn), lambda i, j: (i, j)),
        out_shape=jax.ShapeDtypeStruct((t, d), F32),
        compiler_params=_cp("arbitrary", "arbitrary"),
        name="out_projection",
    )(mix, w_out, x, mods)


def _top2_sum(a, b, c, d):
    hi1, lo1 = jnp.maximum(a, b), jnp.minimum(a, b)
    hi2, lo2 = jnp.maximum(c, d), jnp.minimum(c, d)
    return jnp.maximum(hi1, hi2) + jnp.maximum(jnp.minimum(hi1, hi2), jnp.maximum(lo1, lo2))


def _norm_router_kernel(x_ref, g_ref, sc_ref, sh_ref, wr_ref, rb_ref, hp_ref, idx_ref, wt_ref,
                        *, n_exp, n_grp):
    hf = _norm_mod(x_ref[...], g_ref[...], sc_ref[...], sh_ref[...])
    half = hf.shape[1] // 2
    hp_ref[...] = _pack_pair(hf[:, :half], hf[:, half:])

    lg = lax.dot_general(wr_ref[...].astype(BF16), hf.astype(BF16), _NT, preferred_element_type=F32)
    ex = jnp.exp(lg - lg.max(axis=0, keepdims=True))
    aff = ex * (1.0 / ex.sum(axis=0, keepdims=True))
    sel = aff + rb_ref[...]

    per = n_exp // n_grp
    row = lambda a, e: a[e:e + 1, :]
    assert per == 4, "group score uses a 4-element top-2 network"
    best_s = _top2_sum(*[row(sel, e) for e in range(per)])
    best_g = jnp.zeros_like(best_s, dtype=jnp.int32)
    for g in range(1, n_grp):
        s = _top2_sum(*[row(sel, g * per + e) for e in range(per)])
        upd = s > best_s
        best_g = jnp.where(upd, g, best_g)
        best_s = jnp.where(upd, s, best_s)

    neg = jnp.float32(-jnp.inf)
    picks = []
    for _ in range(2):
        bv = jnp.full_like(best_s, neg)
        bi = jnp.zeros_like(best_g)
        ba = jnp.zeros_like(best_s)
        for e in range(n_exp):
            ok = best_g == (e // per)
            for (pi, _) in picks:
                ok = jnp.logical_and(ok, pi != e)
            val = jnp.where(ok, row(sel, e), neg)
            upd = val > bv
            bv = jnp.where(upd, val, bv)
            bi = jnp.where(upd, e, bi)
            ba = jnp.where(upd, row(aff, e), ba)
        picks.append((bi, ba))
    (i0, a0), (i1, a1) = picks
    inv = 1.0 / (a0 + a1)
    idx_ref[0:1, :] = i0
    idx_ref[1:2, :] = i1
    wt_ref[0:1, :] = a0 * inv
    wt_ref[1:2, :] = a1 * inv


def _norm_router(x, gain, mods, w_router_t, router_bias, layer, dims):
    t, d = x.shape
    n_exp = w_router_t.shape[0]
    rows = _Rows(dims["t_ctx"], dims["dec_seq"], _tile(min(dims["t_ctx"], dims["dec_seq"]), 256))
    tm = rows.tm
    return pl.pallas_call(
        functools.partial(_norm_router_kernel, n_exp=n_exp, n_grp=N_EXPERT_GROUPS),
        grid=(t // tm,),
        in_specs=[
            pl.BlockSpec((tm, d), lambda i: (i, 0)),
            pl.BlockSpec((None, 1, d), lambda i: (layer, 0, 0)),
            _mod_spec(rows, layer, 4, d),
            _mod_spec(rows, layer, 3, d),
            pl.BlockSpec((n_exp, d), lambda i: (0, 0)),
            pl.BlockSpec((n_exp, 1), lambda i: (0, 0)),
        ],
        out_specs=[
            pl.BlockSpec((tm, d // 2), lambda i: (i, 0)),
            pl.BlockSpec((2, tm), lambda i: (0, i)),
            pl.BlockSpec((2, tm), lambda i: (0, i)),
        ],
        out_shape=[
            jax.ShapeDtypeStruct((t, d // 2), U32),
            jax.ShapeDtypeStruct((2, t), jnp.int32),
            jax.ShapeDtypeStruct((2, t), F32),
        ],
        compiler_params=_cp("arbitrary"),
        name="norm_router",
    )(x, gain, mods, mods, w_router_t, router_bias.reshape(n_exp, 1))


def _dispatch_plan(idx, n_exp, item_rows, sub_rows, n_items):
    t = idx.shape[1]
    e_a = idx.reshape(-1)
    onehot = (e_a[:, None] == jnp.arange(n_exp, dtype=jnp.int32)[None, :]).astype(jnp.int32)
    csum = jnp.cumsum(onehot, axis=0)
    counts = csum[-1]
    rank = jnp.take_along_axis(csum, e_a[:, None], axis=1)[:, 0] - 1
    items_e = (counts + item_rows - 1) // item_rows
    item_end = jnp.cumsum(items_e)
    item_start = item_end - items_e
    pos = item_start[e_a] * item_rows + rank
    tok = jnp.tile(jnp.arange(t, dtype=jnp.int32), 2)
    slot_tok = jnp.zeros((n_items * item_rows,), jnp.int32).at[pos].set(tok)
    w = jnp.arange(n_items, dtype=jnp.int32)
    item_exp = jnp.minimum(jnp.searchsorted(item_end, w, side="right"), n_exp - 1).astype(jnp.int32)
    rows = jnp.clip(counts[item_exp] - (w - item_start[item_exp]) * item_rows, 0, item_rows)
    n_sub = jnp.where(w < item_end[-1], (rows + sub_rows - 1) // sub_rows, 0).astype(jnp.int32)
    n_valid = item_end[-1].astype(jnp.int32).reshape(1)
    return slot_tok, pos.reshape(2, t), item_exp, n_sub, n_valid


def _row_copy(src_hbm, dst, sem, src_row, dst_row, n=1):
    return pltpu.make_async_copy(src_hbm.at[pl.ds(src_row, n)], dst.at[pl.ds(dst_row, n)], sem)


GATHER_UNROLL = 8


def _start_row_gather(src_hbm, dst, sem, idx_ref, n_rows):
    def issue(g, carry):
        base = g * GATHER_UNROLL
        for u in range(GATHER_UNROLL):
            _row_copy(src_hbm, dst, sem, idx_ref[0, 0, base + u], base + u).start()
        return carry

    lax.fori_loop(0, n_rows // GATHER_UNROLL, issue, 0)


def _zero_tail(o_ref, first_sub, n_sub_total, sub):
    def body(s, carry):
        o_ref[pl.ds(pl.multiple_of(s * sub, sub), sub), :] = jnp.zeros((sub, o_ref.shape[1]), o_ref.dtype)
        return carry

    lax.fori_loop(first_sub, n_sub_total, body, 0)


def _for_live_rows(n_sub, sub, group, fn):
    big = group * sub
    n_big = n_sub // group

    def big_body(s, carry):
        fn(pl.ds(pl.multiple_of(s * big, big), big))
        return carry

    lax.fori_loop(0, n_big, big_body, 0)

    def small_body(s, carry):
        fn(pl.ds(pl.multiple_of(s * sub, sub), sub))
        return carry

    lax.fori_loop(n_big * group, n_sub, small_body, 0)


def _ffn_up_kernel(te_ref, ns_ref, nv_ref, tok_ref, tok_next_ref, hp_ref, w1_ref, w3_ref, o_ref,
                   xbuf, xl_ref, xr_ref, wb1_ref, wb3_ref, sem, *, sub, group):
    w = pl.program_id(0)
    c = pl.program_id(1)
    nv = nv_ref[0]
    n_sub_total = o_ref.shape[0] // sub
    half = xl_ref.shape[1]

    def start_gather(toks, n_sub):
        _start_row_gather(hp_ref, xbuf, sem, toks, n_sub * sub)

    @pl.when(jnp.logical_and(w == 0, c == 0))
    def _():
        start_gather(tok_ref, ns_ref[0])

    @pl.when(jnp.logical_and(c == 0, w < nv))
    def _():
        def land(s, carry):
            _row_copy(hp_ref, xbuf, sem, 0, 0, sub).wait()
            return carry

        lax.fori_loop(0, ns_ref[w], land, 0)

        def unpack(s, carry):
            rows = pl.ds(pl.multiple_of(s * sub, sub), sub)
            lo, hi = _unpack_pair(xbuf[rows, :])
            xl_ref[rows, :] = lo.astype(BF16)
            xr_ref[rows, :] = hi.astype(BF16)
            return carry

        lax.fori_loop(0, ns_ref[w], unpack, 0)

        @pl.when(w + 1 < nv)
        def _():
            start_gather(tok_next_ref, ns_ref[jnp.minimum(w + 1, ns_ref.shape[0] - 1)])

    @pl.when(w < nv)
    def _():
        wb1_ref[...] = w1_ref[...].astype(BF16)
        wb3_ref[...] = w3_ref[...].astype(BF16)

        def body(rows):
            xl = xl_ref[rows, :]
            xr = xr_ref[rows, :]
            a = (jnp.dot(xl, wb1_ref[:half, :], preferred_element_type=F32)
                 + jnp.dot(xr, wb1_ref[half:, :], preferred_element_type=F32))
            b = (jnp.dot(xl, wb3_ref[:half, :], preferred_element_type=F32)
                 + jnp.dot(xr, wb3_ref[half:, :], preferred_element_type=F32))
            o_ref[rows, :] = (a * jax.nn.sigmoid(a) * b).astype(o_ref.dtype)

        _for_live_rows(ns_ref[w], sub, group, body)
        _zero_tail(o_ref, ns_ref[w], n_sub_total, sub)

    @pl.when(w >= nv)
    def _():
        o_ref[...] = jnp.zeros_like(o_ref)


def _ffn_down_kernel(te_ref, ns_ref, nv_ref, h_ref, wl_ref, wr_ref, o_ref, wbl_ref, wbr_ref,
                     *, sub, group):
    w = pl.program_id(0)
    n_sub_total = o_ref.shape[0] // sub

    @pl.when(w < nv_ref[0])
    def _():
        wbl_ref[...] = wl_ref[...].astype(BF16)
        wbr_ref[...] = wr_ref[...].astype(BF16)

        def body(rows):
            h = h_ref[rows, :]
            yl = jnp.dot(h, wbl_ref[...], preferred_element_type=F32)
            yr = jnp.dot(h, wbr_ref[...], preferred_element_type=F32)
            o_ref[rows, :] = _pack_pair(yl, yr)

        _for_live_rows(ns_ref[w], sub, group, body)
        _zero_tail(o_ref, ns_ref[w], n_sub_total, sub)

    @pl.when(w >= nv_ref[0])
    def _():
        o_ref[...] = jnp.zeros_like(o_ref)


def _expert_ffn(hp, slot_tok, item_exp, n_sub, n_valid, w1, w3, w2, layer, item_rows, sub_rows):
    t, dh = hp.shape
    d = 2 * dh
    f = w1.shape[3]
    n_items = item_exp.shape[0]
    r_total = n_items * item_rows
    tf = _tile(f, 256)
    nf = f // tf
    toks = slot_tok.reshape(n_items, 1, item_rows)

    def live(w, nv):
        return jnp.minimum(w, nv[0] - 1)

    def chunk(w, c, nv, n_chunks):
        return jnp.where(w < nv[0], c, n_chunks - 1)

    hid = pl.pallas_call(
        functools.partial(_ffn_up_kernel, sub=sub_rows, group=MOE_GROUP_SUBS),
        grid_spec=pltpu.PrefetchScalarGridSpec(
            num_scalar_prefetch=3,
            grid=(n_items, nf),
            in_specs=[
                pl.BlockSpec((1, 1, item_rows), lambda w, c, te, ns, nv: (w, 0, 0),
                             memory_space=pltpu.SMEM),
                pl.BlockSpec((1, 1, item_rows),
                             lambda w, c, te, ns, nv: (jnp.minimum(w + 1, n_items - 1), 0, 0),
                             memory_space=pltpu.SMEM),
                pl.BlockSpec(memory_space=pl.ANY),
                pl.BlockSpec((None, None, d, tf),
                             lambda w, c, te, ns, nv: (layer, te[live(w, nv)], 0, chunk(w, c, nv, nf))),
                pl.BlockSpec((None, None, d, tf),
                             lambda w, c, te, ns, nv: (layer, te[live(w, nv)], 0, chunk(w, c, nv, nf))),
            ],
            out_specs=pl.BlockSpec((item_rows, tf), lambda w, c, te, ns, nv: (w, c)),
            scratch_shapes=[
                pltpu.VMEM((item_rows, dh), U32),
                pltpu.VMEM((item_rows, dh), BF16),
                pltpu.VMEM((item_rows, dh), BF16),
                pltpu.VMEM((d, tf), BF16),
                pltpu.VMEM((d, tf), BF16),
                pltpu.SemaphoreType.DMA,
            ],
        ),
        out_shape=jax.ShapeDtypeStruct((r_total, f), BF16),
        compiler_params=_cp("arbitrary", "arbitrary"),
        name="expert_ffn_up",
    )(item_exp, n_sub, n_valid, toks, toks, hp, w1, w3)

    tn = _tile(dh, 512)
    nn = dh // tn
    return pl.pallas_call(
        functools.partial(_ffn_down_kernel, sub=sub_rows, group=MOE_GROUP_SUBS),
        grid_spec=pltpu.PrefetchScalarGridSpec(
            num_scalar_prefetch=3,
            grid=(n_items, nn),
            in_specs=[
                pl.BlockSpec((item_rows, f), lambda w, c, te, ns, nv: (live(w, nv), 0)),
                pl.BlockSpec((None, None, f, tn),
                             lambda w, c, te, ns, nv: (layer, te[live(w, nv)], 0, chunk(w, c, nv, nn))),
                pl.BlockSpec((None, None, f, tn),
                             lambda w, c, te, ns, nv: (layer, te[live(w, nv)], 0,
                                                       nn + chunk(w, c, nv, nn))),
            ],
            out_specs=pl.BlockSpec((item_rows, tn), lambda w, c, te, ns, nv: (w, c)),
            scratch_shapes=[pltpu.VMEM((f, tn), BF16), pltpu.VMEM((f, tn), BF16)],
        ),
        out_shape=jax.ShapeDtypeStruct((r_total, dh), U32),
        compiler_params=_cp("arbitrary", "arbitrary"),
        name="expert_ffn_down",
    )(item_exp, n_sub, n_valid, hid, w2, w2)


def _combine_kernel(pos_ref, pos_next_ref, y_ref, x_ref, g_ref, wt_ref, *rest, tm, with_next_norm,
                    n_ctx_tiles):
    if with_next_norm:
        gn_ref, scn_ref, shn_ref, o_ref, h_ref, ybuf, sem = rest
    else:
        o_ref, o_lat_ref, ybuf, sem = rest
    i = pl.program_id(0)
    slot = i % 2

    def start_gather(poss, dst_slot):
        _start_row_gather(y_ref, ybuf.at[dst_slot], sem.at[dst_slot], poss, 2 * tm)

    @pl.when(i == 0)
    def _():
        start_gather(pos_ref, 0)

    @pl.when(i + 1 < pl.num_programs(0))
    def _():
        start_gather(pos_next_ref, 1 - slot)

    _row_copy(y_ref, ybuf.at[slot], sem.at[slot], 0, 0, 2 * tm).wait()

    half = ybuf.shape[2]
    w0 = wt_ref[:, 0:1]
    w1 = wt_ref[:, 1:2]
    l0, r0 = _unpack_pair(ybuf[slot, 0:tm, :])
    l1, r1 = _unpack_pair(ybuf[slot, tm:2 * tm, :])
    ol = x_ref[:, :half] + g_ref[:, :half] * (w0 * l0 + w1 * l1)
    orr = x_ref[:, half:] + g_ref[:, half:] * (w0 * r0 + w1 * r1)

    def store(dst_ref):
        dst_ref[:, :half] = ol
        dst_ref[:, half:] = orr

    if with_next_norm:
        store(o_ref)
    else:
        @pl.when(i < n_ctx_tiles)
        def _():
            store(o_ref)

        @pl.when(i >= n_ctx_tiles)
        def _():
            store(o_lat_ref)

    if with_next_norm:
        ms = (jnp.sum(ol * ol, axis=-1, keepdims=True)
              + jnp.sum(orr * orr, axis=-1, keepdims=True)) * (1.0 / (2 * half))
        inv = lax.rsqrt(ms + EPS)
        for sl, part in ((slice(0, half), ol), (slice(half, 2 * half), orr)):
            yn = part * inv * gn_ref[:, sl]
            h_ref[:, sl] = (yn * (1.0 + scn_ref[:, sl]) + shn_ref[:, sl]).astype(h_ref.dtype)


def _combine(y, pos, wts, x, mods, layer, dims, next_gain=None):
    t, d = x.shape
    rows = _Rows(dims["t_ctx"], dims["dec_seq"], _tile(min(dims["t_ctx"], dims["dec_seq"]), 256))
    tm = rows.tm
    nt = t // tm
    with_next = next_gain is not None
    pos_tiles = pos.reshape(2, nt, tm).transpose(1, 0, 2).reshape(nt, 1, 2 * tm)
    row_spec = pl.BlockSpec((tm, d), lambda i: (i, 0))
    in_specs = [
        pl.BlockSpec((1, 1, 2 * tm), lambda i: (i, 0, 0), memory_space=pltpu.SMEM),
        pl.BlockSpec((1, 1, 2 * tm), lambda i: (jnp.minimum(i + 1, nt - 1), 0, 0),
                     memory_space=pltpu.SMEM),
        pl.BlockSpec(memory_space=pl.ANY),
        row_spec,
        _mod_spec(rows, layer, 5, d),
        pl.BlockSpec((tm, 2), lambda i: (i, 0)),
    ]
    args = [pos_tiles, pos_tiles, y, x, mods, wts.T]
    nct = rows.n_ctx_tiles
    if with_next:
        in_specs += [pl.BlockSpec((None, 1, d), lambda i: (layer + 1, 0, 0)),
                     _mod_spec(rows, layer + 1, 1, d), _mod_spec(rows, layer + 1, 0, d)]
        args += [next_gain, mods, mods]
        out_specs = [row_spec, row_spec]
        out_shape = [jax.ShapeDtypeStruct((t, d), F32), jax.ShapeDtypeStruct((t, d), BF16)]
    else:
        out_specs = [pl.BlockSpec((tm, d), lambda i: (jnp.minimum(i, nct - 1), 0)),
                     pl.BlockSpec((tm, d), lambda i: (jnp.maximum(i - nct, 0), 0))]
        out_shape = [jax.ShapeDtypeStruct((dims["t_ctx"], d), F32),
                     jax.ShapeDtypeStruct((t - dims["t_ctx"], d), F32)]
    return pl.pallas_call(
        functools.partial(_combine_kernel, tm=tm, with_next_norm=with_next, n_ctx_tiles=nct),
        grid=(nt,),
        in_specs=in_specs,
        out_specs=out_specs,
        out_shape=out_shape,
        scratch_shapes=[pltpu.VMEM((2, 2 * tm, d // 2), U32), pltpu.SemaphoreType.DMA((2,))],
        compiler_params=_cp("arbitrary"),
        name="moe_combine",
    )(*args)


MOE_ITEM_ROWS = 1536
MOE_SUB_ROWS = 128
MOE_GROUP_SUBS = 4


def kernel(x_prompt, x_sample, cache_k, cache_v, c, c_ctx, w_ada, b_ada, norm_mix, norm_ffn, w_in, q_norm, k_norm, w_fourier, w_out, w_router, router_bias, w1, w3, w2):
    batch, seq, d = x_prompt.shape
    dec_batch, dec_seq, _ = x_sample.shape
    depth = w_in.shape[0]
    past, nkv, hd = cache_k.shape[2], cache_k.shape[3], cache_k.shape[4]
    ng, gd = w_fourier.shape[1], w_fourier.shape[2]
    fw = ng * gd
    kvw = nkv * hd
    aw = w_out.shape[1] - fw
    n_exp = w_router.shape[1]
    t_ctx = batch * seq
    t_lat = dec_batch * dec_seq
    t = t_ctx + t_lat
    assert hd == LANE and dec_batch + 1 <= COND_ROWS and w_in.shape[2] == aw + 2 * kvw + fw
    dims = dict(hd=hd, aw=aw, kvw=kvw, fw=fw, nkv=nkv, qpk=aw // kvw, seq=seq, dec_seq=dec_seq,
                t_ctx=t_ctx, past=past)

    cond = jnp.zeros((COND_ROWS, d), F32).at[0].set(c_ctx).at[1:1 + dec_batch].set(c)
    mods = _modulation(cond, w_ada, b_ada).reshape(depth, COND_ROWS, N_MOD, 1, d)

    rope = _rope_tables(dec_seq, hd)
    cs_c, sn_c = _dft_tables(seq)
    cs_l, sn_l = _dft_tables(dec_seq)
    cs_g, sn_g = _dft_tables(gd)
    bf_const = lambda a: jnp.asarray(a.astype(np.float32).astype(BF16))
    dch = bf_const(np.concatenate([cs_g, sn_g], axis=1))
    dft_ctx = (bf_const(np.concatenate([cs_c, -sn_c], axis=1)), dch)
    dft_lat = (bf_const(np.concatenate([cs_l, -sn_l], axis=1)), dch)

    cache_k4 = cache_k.reshape(dec_batch, depth, past, kvw)
    cache_v4 = cache_v.reshape(dec_batch, depth, past, kvw)
    w_router_t = w_router.T
    n_moe_items = (2 * t) // MOE_ITEM_ROWS + n_exp
    mix = jnp.zeros((t, aw + fw), BF16)
    norm_mix3 = norm_mix.reshape(depth, 1, d)
    norm_ffn3 = norm_ffn.reshape(depth, 1, d)
    q_norm3 = q_norm.reshape(depth, 1, hd)
    k_norm3 = k_norm.reshape(depth, 1, hd)

    x, h = _stack_norm_modulate(x_prompt.reshape(t_ctx, d), x_sample.reshape(t_lat, d),
                                norm_mix3, mods, 0, dec_seq)
    new_k = jnp.zeros((batch, depth, seq, kvw), F32)
    new_v = jnp.zeros((batch, depth, seq, kvw), F32)
    for l in range(depth):
        q, k, v, u = _project(h, w_in, q_norm3, k_norm3, rope, l, dims)
        new_k, new_v = _store_context_cache(k, v, new_k, new_v, l, dims)
        mix = _attention(q, k, v, cache_k4, cache_v4, mix, l, dims)
        mix = _fourier(u, w_fourier, mix, l, seq, 0, batch, dft_ctx)
        mix = _fourier(u, w_fourier, mix, l, dec_seq, t_ctx, dec_batch, dft_lat)
        x = _out_project(mix, w_out, x, mods, l, dims)

        hp, idx, wts = _norm_router(x, norm_ffn3, mods, w_router_t, router_bias, l, dims)
        slot_tok, pos, item_exp, n_sub, n_valid = _dispatch_plan(
            idx, n_exp, MOE_ITEM_ROWS, MOE_SUB_ROWS, n_moe_items)
        y = _expert_ffn(hp, slot_tok, item_exp, n_sub, n_valid, w1, w3, w2, l,
                        MOE_ITEM_ROWS, MOE_SUB_ROWS)
        if l + 1 < depth:
            x, h = _combine(y, pos, wts, x, mods, l, dims, next_gain=norm_mix3)
        else:
            y_ctx, y_lat = _combine(y, pos, wts, x, mods, l, dims)

    y_prompt = y_ctx.reshape(batch, seq, d)
    y_sample = y_lat.reshape(dec_batch, dec_seq, d)
    cache_shape = (batch, depth, seq, nkv, hd)
    return (y_prompt, y_sample, new_k.reshape(cache_shape), new_v.reshape(cache_shape))
```
